```python
import math
import jax, jax.numpy as jnp
from jax import lax
import numpy as np

D_MODEL = 1024
BATCH = 8
SEQ = 4096
DEPTH = 4

N_A_LAYERS = DEPTH // 2
N_B_LAYERS = DEPTH - N_A_LAYERS
N_DENSE = (DEPTH + 1) // 2
N_MOE = DEPTH // 2

RK_HEAD_DIM = 64
RK_HEADS = D_MODEL // RK_HEAD_DIM
DECAY_LORA = max(32, int(round(1.8 * D_MODEL ** 0.5 / 32)) * 32)
AAA_LORA = max(32, int(round(1.8 * D_MODEL ** 0.5 / 32)) * 32)
MV_LORA = max(32, int(round(1.3 * D_MODEL ** 0.5 / 32)) * 32)
GATE_LORA = max(32, int(round(0.6 * D_MODEL ** 0.8 / 32)) * 32)
GN_EPS = 64e-5

FOX_HEAD_DIM = 64
FOX_HEADS = D_MODEL // FOX_HEAD_DIM
Q_BLOCK = 128

FFN_DIM = 256 * int(math.ceil(8 * D_MODEL / 3 / 256))
N_EXPERTS = 8
TOP_K = 2
EXPERT_DIM = int(3.5 * D_MODEL)
MOE_BLOCK = 512

RMS_EPS = 1e-6

kernel_name = "yoco_rwkv7_fox_moe_hybrid"


def rms_norm(x, g):
    xf = x.astype(jnp.float32)
    y = xf * lax.rsqrt(jnp.mean(xf * xf, axis=-1, keepdims=True) + RMS_EPS)
    return (y * g).astype(x.dtype)


def wkv7_scan(r, w, k, v, a, b):
    B, S, H, N = r.shape

    def step(state, inp):
        r_t, w_t, k_t, v_t, a_t, b_t = inp
        sa = jnp.einsum('bhij,bhj->bhi', state, a_t)
        state = (state * w_t[:, :, None, :]
                 + sa[..., None] * b_t[:, :, None, :]
                 + v_t[..., None] * k_t[:, :, None, :])
        return state, jnp.einsum('bhij,bhj->bhi', state, r_t)

    seq_first = lambda t: jnp.moveaxis(t, 1, 0)
    state0 = jnp.zeros((B, H, N, N), jnp.float32)
    _, y = lax.scan(step, state0, (seq_first(r), seq_first(w), seq_first(k),
                                   seq_first(v), seq_first(a), seq_first(b)))
    return jnp.moveaxis(y, 0, 1)


def rwkv7_time_mix(h, v_first, mu, w_r, w_k, w_v, w_o, w0, w1, w2, a0, a1, a2,
                   g1, g2, k_k, k_a, r_k, lnx_w, lnx_b, vres):
    B, S, D = h.shape
    xx = jnp.pad(h, ((0, 0), (1, 0), (0, 0)))[:, :-1] - h
    xr, xw, xk, xv, xa, xg = [h + xx * mu[i] for i in range(6)]
    r = xr @ w_r
    k = xk @ w_k
    v = xv @ w_v
    w = -jax.nn.softplus(-(w0 + jnp.tanh(xw @ w1) @ w2)) - 0.5
    a = jax.nn.sigmoid(a0 + (xa @ a1) @ a2)
    g = jax.nn.sigmoid(xg @ g1) @ g2
    if vres is None:
        v_first = v
    else:
        v0, v1, v2 = vres
        v = v + (v_first - v) * jax.nn.sigmoid(v0 + (xv @ v1) @ v2)
    heads = lambda t: t.reshape(B, S, RK_HEADS, RK_HEAD_DIM).astype(jnp.float32)
    kk = heads(k * k_k)
    kk = kk / jnp.maximum(jnp.linalg.norm(kk, axis=-1, keepdims=True), 1e-12)
    k = k * (1 + (a - 1) * k_a)
    rh, kh, vh, ah = heads(r), heads(k), heads(v), heads(a)
    decay = jnp.exp(-jnp.exp(heads(w)))
    y = wkv7_scan(rh, decay, kh, vh, -kk, kk * ah)
    mean = jnp.mean(y, axis=-1, keepdims=True)
    var = jnp.mean(jnp.square(y - mean), axis=-1, keepdims=True)
    y = ((y - mean) * lax.rsqrt(var + GN_EPS)).reshape(B, S, D) * lnx_w + lnx_b
    bonus = jnp.sum(rh * kh * r_k, axis=-1, keepdims=True) * vh
    y = y + bonus.reshape(B, S, D)
    return (y.astype(h.dtype) * g) @ w_o, v_first


def shared_fox_kv(x, kv_norm, w_kvf, b_f, k_norm):
    B, S, D = x.shape
    kvf = rms_norm(x, kv_norm) @ w_kvf
    k = rms_norm(kvf[..., :D].reshape(B, S, FOX_HEADS, FOX_HEAD_DIM), k_norm)
    v = kvf[..., D:2 * D].reshape(B, S, FOX_HEADS, FOX_HEAD_DIM)
    log_f = jax.nn.log_sigmoid((kvf[..., 2 * D:] + b_f).astype(jnp.float32))
    dcum = jnp.cumsum(log_f, axis=1)
    return (k.transpose(0, 2, 1, 3), v.transpose(0, 2, 1, 3), dcum.transpose(0, 2, 1))


def fox_attention(h, w_qg, q_norm, w_o, k, v, dcum):
    B, S, D = h.shape
    qg = h @ w_qg
    q = rms_norm(qg[..., :D].reshape(B, S, FOX_HEADS, FOX_HEAD_DIM), q_norm)
    q = q.transpose(0, 2, 1, 3)
    og = qg[..., D:]
    scale = FOX_HEAD_DIM ** -0.5
    outs = []
    for i in range(S // Q_BLOCK):
        lo, hi = i * Q_BLOCK, (i + 1) * Q_BLOCK
        logits = jnp.einsum('bhqd,bhkd->bhqk', q[:, :, lo:hi], k[:, :, :hi]).astype(jnp.float32) * scale
        logits = logits + dcum[:, :, lo:hi, None] - dcum[:, :, None, :hi]
        causal = jnp.arange(hi)[None, :] <= jnp.arange(lo, hi)[:, None]
        logits = jnp.where(causal, logits, -jnp.inf)
        p = jax.nn.softmax(logits, axis=-1)
        outs.append(jnp.einsum('bhqk,bhkd->bhqd', p.astype(v.dtype), v[:, :, :hi]))
    o = jnp.concatenate(outs, axis=2).transpose(0, 2, 1, 3).reshape(B, S, D)
    return (o * jax.nn.sigmoid(og)) @ w_o


def swiglu(h, w1, w3, w2):
    return (jax.nn.silu(h @ w1) * (h @ w3)) @ w2


def moe_swiglu(h, w_router, w1, w3, w2):
    B, S, D = h.shape
    t = h.reshape(-1, D)
    n = t.shape[0]
    logits = (t @ w_router).astype(jnp.float32)
    top_val, top_idx = lax.top_k(logits, TOP_K)
    top_w = jax.nn.softmax(top_val, axis=-1)
    flat_e = top_idx.reshape(-1)
    flat_tok = jnp.arange(n * TOP_K, dtype=jnp.int32) // TOP_K
    flat_w = top_w.reshape(-1)
    order = jnp.argsort(flat_e)
    e_sorted = flat_e[order]
    counts = jnp.bincount(flat_e, length=N_EXPERTS)
    padded = ((counts + MOE_BLOCK - 1) // MOE_BLOCK) * MOE_BLOCK
    pend = jnp.cumsum(padded)
    pstart = pend - padded
    ustart = jnp.cumsum(counts) - counts
    dest = pstart[e_sorted] + jnp.arange(n * TOP_K) - ustart[e_sorted]
    n_blocks = -(-(n * TOP_K) // MOE_BLOCK) + N_EXPERTS
    cap = n_blocks * MOE_BLOCK
    rows_tok = jnp.zeros((cap,), jnp.int32).at[dest].set(flat_tok[order])
    rows_w = jnp.zeros((cap,), jnp.float32).at[dest].set(flat_w[order])
    block_expert = jnp.minimum(
        jnp.searchsorted(pend, jnp.arange(n_blocks) * MOE_BLOCK, side='right'), N_EXPERTS - 1)
    xs = t[rows_tok].reshape(n_blocks, MOE_BLOCK, D)

    def expert_block(args):
        xb, e = args
        return (jax.nn.silu(xb @ w1[e]) * (xb @ w3[e])) @ w2[e]

    yb = lax.map(expert_block, (xs, block_expert)).reshape(cap, D)
    y = jax.ops.segment_sum(yb * rows_w[:, None], rows_tok, num_segments=n)
    return y.astype(h.dtype).reshape(B, S, D)


def setup_inputs(seed: int = 0) -> dict:
    key = jax.random.key(seed)
    ks = iter(jax.random.split(key, 64))
    f32 = jnp.float32
    D = D_MODEL
    na, nb, nd, nm = N_A_LAYERS, N_B_LAYERS, N_DENSE, N_MOE

    def normal(shape, scale):
        return jax.random.normal(next(ks), shape, f32) * scale

    def gain(shape):
        return 1.0 + normal(shape, 0.1)

    return {
        "x": normal((BATCH, SEQ, D), 1.0),
        "norm_mix": gain((DEPTH, D)),
        "norm_ffn": gain((DEPTH, D)),
        "rk_mu": jax.random.uniform(next(ks), (na, 6, D), f32),
        "rk_w_r": normal((na, D, D), D ** -0.5),
        "rk_w_k": normal((na, D, D), D ** -0.5),
        "rk_w_v": normal((na, D, D), D ** -0.5),
        "rk_w_o": normal((na, D, D), D ** -0.5),
        "rk_w0": jax.random.uniform(next(ks), (na, D), f32, -6.0, -1.0),
        "rk_w1": normal((na, D, DECAY_LORA), D ** -0.5),
        "rk_w2": normal((na, DECAY_LORA, D), 0.1 * DECAY_LORA ** -0.5),
        "rk_a0": normal((na, D), 0.1),
        "rk_a1": normal((na, D, AAA_LORA), D ** -0.5),
        "rk_a2": normal((na, AAA_LORA, D), 0.1 * AAA_LORA ** -0.5),
        "rk_v0": normal((na - 1, D), 0.1),
        "rk_v1": normal((na - 1, D, MV_LORA), D ** -0.5),
        "rk_v2": normal((na - 1, MV_LORA, D), 0.1 * MV_LORA ** -0.5),
        "rk_g1": normal((na, D, GATE_LORA), D ** -0.5),
        "rk_g2": normal((na, GATE_LORA, D), GATE_LORA ** -0.5),
        "rk_k_k": 0.85 + normal((na, D), 0.05),
        "rk_k_a": 1.0 + normal((na, D), 0.05),
        "rk_r_k": normal((na, RK_HEADS, RK_HEAD_DIM), 0.1),
        "rk_lnx_w": gain((na, D)),
        "rk_lnx_b": normal((na, D), 0.01),
        "kv_norm": gain((D,)),
        "w_kvf": normal((D, 2 * D + FOX_HEADS), D ** -0.5),
        "b_f": 2.0 + normal((FOX_HEADS,), 0.5),
        "k_norm": gain((FOX_HEAD_DIM,)),
        "fx_w_qg": normal((nb, D, 2 * D), D ** -0.5),
        "fx_q_norm": gain((nb, FOX_HEAD_DIM)),
        "fx_w_o": normal((nb, D, D), D ** -0.5),
        "ffn_w1": normal((nd, D, FFN_DIM), D ** -0.5),
        "ffn_w3": normal((nd, D, FFN_DIM), D ** -0.5),
        "ffn_w2": normal((nd, FFN_DIM, D), FFN_DIM ** -0.5),
        "moe_router": normal((nm, D, N_EXPERTS), D ** -0.5),
        "moe_w1": normal((nm, N_EXPERTS, D, EXPERT_DIM), D ** -0.5),
        "moe_w3": normal((nm, N_EXPERTS, D, EXPERT_DIM), D ** -0.5),
        "moe_w2": normal((nm, N_EXPERTS, EXPERT_DIM, D), EXPERT_DIM ** -0.5),
    }


def reference(x, norm_mix, norm_ffn, rk_mu, rk_w_r, rk_w_k, rk_w_v, rk_w_o,
              rk_w0, rk_w1, rk_w2, rk_a0, rk_a1, rk_a2, rk_v0, rk_v1, rk_v2,
              rk_g1, rk_g2, rk_k_k, rk_k_a, rk_r_k, rk_lnx_w, rk_lnx_b,
              kv_norm, w_kvf, b_f, k_norm, fx_w_qg, fx_q_norm, fx_w_o,
              ffn_w1, ffn_w3, ffn_w2, moe_router, moe_w1, moe_w3, moe_w2):
    v_first = None
    shared_kv = None
    for l in range(DEPTH):
        h = rms_norm(x, norm_mix[l])
        if l < N_A_LAYERS:
            vres = None if l == 0 else (rk_v0[l - 1], rk_v1[l - 1], rk_v2[l - 1])
            out, v_first = rwkv7_time_mix(
                h, v_first, rk_mu[l], rk_w_r[l], rk_w_k[l], rk_w_v[l], rk_w_o[l],
                rk_w0[l], rk_w1[l], rk_w2[l], rk_a0[l], rk_a1[l], rk_a2[l],
                rk_g1[l], rk_g2[l], rk_k_k[l], rk_k_a[l], rk_r_k[l],
                rk_lnx_w[l], rk_lnx_b[l], vres)
        else:
            if shared_kv is None:
                shared_kv = shared_fox_kv(x, kv_norm, w_kvf, b_f, k_norm)
            k_sh, v_sh, dcum_sh = shared_kv
            j = l - N_A_LAYERS
            out = fox_attention(h, fx_w_qg[j], fx_q_norm[j], fx_w_o[j], k_sh, v_sh, dcum_sh)
        x = x + out
        h2 = rms_norm(x, norm_ffn[l])
        if l % 2 == 0:
            i = l // 2
            x = x + swiglu(h2, ffn_w1[i], ffn_w3[i], ffn_w2[i])
        else:
            i = l // 2
            x = x + moe_swiglu(h2, moe_router[i], moe_w1[i], moe_w3[i], moe_w2[i])
    return x
```

```python
import functools

import jax
import jax.numpy as jnp
from jax import lax
from jax.experimental import pallas as pl
from jax.experimental.pallas import tpu as pltpu

F32 = jnp.float32
BF16 = jnp.bfloat16
HI = lax.Precision.HIGHEST

D_MODEL = 1024
HEAD_DIM = 64
N_HEADS = D_MODEL // HEAD_DIM
LANES = 128
HEAD_PAIRS = D_MODEL // LANES
N_EXPERTS = 8
RMS_EPS = 1e-6
GN_EPS = 64e-5
VMEM_LIMIT_BYTES = 56 * 1024 * 1024

WKV_CHUNK = 64
WKV_TILE = 512
ATT_TQ = 256
ATT_TK = 256
MOE_BM = 1024
GATHER_ROWS = 512


def _params(*sem):
    return pltpu.CompilerParams(dimension_semantics=sem, vmem_limit_bytes=VMEM_LIMIT_BYTES)


def _dot(a, b):
    return jnp.dot(a, b, preferred_element_type=F32)


def _dot_hi(a, b):
    return jnp.dot(a, b, preferred_element_type=F32, precision=HI)


def _dot_nt(a, b):
    return lax.dot_general(a, b, (((1,), (1,)), ((), ())), preferred_element_type=F32)


def _dot_tn(a, b):
    return lax.dot_general(a, b, (((0,), (0,)), ((), ())), preferred_element_type=F32)


def _rms(x, g):
    return x * lax.rsqrt(jnp.mean(x * x, axis=-1, keepdims=True) + RMS_EPS) * g


def _sigmoid(x):
    return 1.0 / (1.0 + jnp.exp(-x))


def _pair_ones():
    ri = lax.broadcasted_iota(jnp.int32, (LANES, LANES), 0)
    ci = lax.broadcasted_iota(jnp.int32, (LANES, LANES), 1)
    return ((ri < HEAD_DIM) == (ci < HEAD_DIM)).astype(F32)


def _head_rms(x, gain_row):
    ones = _pair_ones()
    parts = []
    for p in range(HEAD_PAIRS):
        xs = x[:, p * LANES:(p + 1) * LANES]
        ms = _dot_hi(xs * xs, ones) * (1.0 / HEAD_DIM)
        parts.append(xs * lax.rsqrt(ms + RMS_EPS))
    return jnp.concatenate(parts, axis=1) * gain_row


def _rwkv_proj_kernel(has_vres, *refs):
    if has_vres:
        (x_ref, gn_ref, mu_ref, wr_ref, wk_ref, wv_ref, w0_ref, w1_ref, w2_ref, a0_ref, a1_ref,
         a2_ref, g1_ref, g2_ref, v0_ref, v1_ref, v2_ref, vf_ref,
         r_out, lw_out, k_out, v_out, a_out, g_out, carry) = refs
    else:
        (x_ref, gn_ref, mu_ref, wr_ref, wk_ref, wv_ref, w0_ref, w1_ref, w2_ref, a0_ref, a1_ref,
         a2_ref, g1_ref, g2_ref,
         r_out, lw_out, k_out, v_out, a_out, g_out, carry) = refs
    s = pl.program_id(1)
    tm = x_ref.shape[1]

    @pl.when(s == 0)
    def _():
        carry[...] = jnp.zeros_like(carry)

    h = _rms(x_ref[0], gn_ref[...])
    prev_last = carry[...]
    carry[...] = h[tm - 1:tm, :]
    row = lax.broadcasted_iota(jnp.int32, (tm, 1), 0)
    xx = jnp.where(row == 0, prev_last, pltpu.roll(h, 1, 0)) - h

    def mix(i):
        return (h + xx * mu_ref[i:i + 1, :]).astype(BF16)

    xr, xw, xk, xv, xa, xg = [mix(i) for i in range(6)]
    r_out[0] = _dot(xr, wr_ref[...]).astype(BF16)
    k_out[0] = _dot(xk, wk_ref[...]).astype(BF16)
    v = _dot(xv, wv_ref[...])
    wl = w0_ref[...] + _dot(jnp.tanh(_dot(xw, w1_ref[...])).astype(BF16), w2_ref[...])
    w = -(jnp.maximum(-wl, 0.0) + jnp.log(1.0 + jnp.exp(-jnp.abs(wl)))) - 0.5
    lw_out[0] = -jnp.exp(w)
    a_out[0] = _sigmoid(a0_ref[...] + _dot(_dot(xa, a1_ref[...]).astype(BF16), a2_ref[...])).astype(BF16)
    g_out[0] = _dot(_sigmoid(_dot(xg, g1_ref[...])).astype(BF16), g2_ref[...]).astype(BF16)
    if has_vres:
        gate = _sigmoid(v0_ref[...] + _dot(_dot(xv, v1_ref[...]).astype(BF16), v2_ref[...]))
        v = v + (vf_ref[0].astype(F32) - v) * gate
    v_out[0] = v.astype(BF16)


def _rwkv_proj(x, gn, mu, wr, wk, wv, w0, w1, w2, a0, a1, a2, g1, g2, vres, v_first, tm=256):
    B, S, D = x.shape
    has_vres = vres is not None
    row = lambda a: a.reshape(1, -1)
    full = lambda a: pl.BlockSpec(a.shape, lambda b, s: (0,) * a.ndim)
    tok = pl.BlockSpec((1, tm, D), lambda b, s: (b, s, 0))
    ins = [x, row(gn), mu, wr.astype(BF16), wk.astype(BF16), wv.astype(BF16), row(w0),
           w1.astype(BF16), w2.astype(BF16), row(a0), a1.astype(BF16), a2.astype(BF16),
           g1.astype(BF16), g2.astype(BF16)]
    if has_vres:
        v0, v1, v2 = vres
        ins += [row(v0), v1.astype(BF16), v2.astype(BF16)]
    specs = [tok] + [full(a) for a in ins[1:]]
    if has_vres:
        ins.append(v_first)
        specs.append(tok)
    out_shape = [jax.ShapeDtypeStruct((B, S, D), dt) for dt in (BF16, F32, BF16, BF16, BF16, BF16)]
    return pl.pallas_call(
        functools.partial(_rwkv_proj_kernel, has_vres),
        grid=(B, S // tm),
        in_specs=specs,
        out_specs=[tok] * 6,
        out_shape=out_shape,
        scratch_shapes=[pltpu.VMEM((1, D), F32)],
        compiler_params=_params("arbitrary", "arbitrary"),
        name="rwkv_proj_vres" if has_vres else "rwkv_proj",
    )(*ins)


def _wkv_kernel(r_ref, lw_ref, k_ref, v_ref, a_ref, g_ref, kk_ref, ka_ref, rk_ref, lnw_ref, lnb_ref,
                z_ref, state):
    C = WKV_CHUNK
    P = 2 * C
    tt = r_ref.shape[1]

    @pl.when(pl.program_id(2) == 0)
    def _():
        state[...] = jnp.zeros_like(state)

    lane = lax.broadcasted_iota(jnp.int32, (C, LANES), 1)
    m0 = (lane < HEAD_DIM).astype(F32)
    m1 = 1.0 - m0
    ri = lax.broadcasted_iota(jnp.int32, (P, P), 0)
    ci = lax.broadcasted_iota(jnp.int32, (P, P), 1)
    strict = ri > ci
    incl = ri >= ci
    eye = (ri == ci).astype(F32)
    tri = (lax.broadcasted_iota(jnp.int32, (C, C), 0) >= lax.broadcasted_iota(jnp.int32, (C, C), 1)).astype(F32)
    ones = _pair_ones()
    k_k, k_a, r_k = kk_ref[...], ka_ref[...], rk_ref[...]
    ln_w, ln_b = lnw_ref[...], lnb_ref[...]

    def stack(x):
        return jnp.concatenate([x * m0, x * m1], axis=0)

    def chunk(c, carry):
        sl = pl.ds(pl.multiple_of(c * C, C), C)
        r = r_ref[0, sl, :].astype(F32)
        k = k_ref[0, sl, :].astype(F32)
        v = v_ref[0, sl, :].astype(F32)
        a = a_ref[0, sl, :].astype(F32)
        lw = lw_ref[0, sl, :]
        kk = k * k_k
        kk = kk * lax.rsqrt(jnp.maximum(_dot_hi(kk * kk, ones), 1e-24))
        k2 = k * (1.0 + (a - 1.0) * k_a)
        bv = kk * a
        cum = _dot_hi(tri, lw)
        cum_last = cum[C - 1:C, :]
        g_inv = jnp.exp(-cum)
        g_end = jnp.exp(cum_last - cum)
        a_s = stack(-kk * jnp.exp(cum - lw)).astype(BF16)
        r_s = stack(r * jnp.exp(cum)).astype(BF16)
        b_s = stack(bv * g_inv).astype(BF16)
        k_s = stack(k2 * g_inv).astype(BF16)
        v_s = stack(v).astype(BF16)
        s0 = state[...]
        s0b = s0.astype(BF16)

        n_ab = jnp.where(strict, _dot_nt(a_s, b_s), 0.0)
        a_ak = jnp.where(strict, _dot_nt(a_s, k_s), 0.0)
        a_rb = jnp.where(incl, _dot_nt(r_s, b_s), 0.0)
        a_rk = jnp.where(incl, _dot_nt(r_s, k_s), 0.0)
        inv = eye + n_ab
        pw = n_ab
        for _ in range(5):
            pw = _dot_hi(pw, pw)
            inv = inv + _dot_hi(inv, pw)
        rhs = _dot_nt(a_s, s0b) + _dot(a_ak.astype(BF16), v_s)
        u = _dot_hi(inv, rhs)
        u_b = u.astype(BF16)
        y_s = _dot_nt(r_s, s0b) + _dot(a_rb.astype(BF16), u_b) + _dot(a_rk.astype(BF16), v_s)
        y = y_s[:C] + y_s[C:]
        state[...] = (s0 * jnp.exp(cum_last)
                      + _dot_tn(u_b, stack(bv * g_end).astype(BF16))
                      + _dot_tn(v_s, stack(k2 * g_end).astype(BF16)))

        mean = _dot_hi(y, ones) * (1.0 / HEAD_DIM)
        yc = y - mean
        var = _dot_hi(yc * yc, ones) * (1.0 / HEAD_DIM)
        yn = yc * lax.rsqrt(var + GN_EPS) * ln_w + ln_b
        bonus = _dot_hi(r * k2 * r_k, ones) * v
        z_ref[0, sl, :] = ((yn + bonus) * g_ref[0, sl, :].astype(F32)).astype(BF16)
        return carry

    lax.fori_loop(0, tt // C, chunk, 0)


def _wkv(r, lw, k, v, a, g, k_k, k_a, r_k, ln_w, ln_b):
    B, S, D = r.shape
    tt = min(WKV_TILE, S)
    tok = pl.BlockSpec((1, tt, LANES), lambda b, p, t: (b, t, p))
    par = pl.BlockSpec((1, LANES), lambda b, p, t: (0, p))
    row = lambda x: x.reshape(1, D)
    return pl.pallas_call(
        _wkv_kernel,
        grid=(B, HEAD_PAIRS, S // tt),
        in_specs=[tok] * 6 + [par] * 5,
        out_specs=tok,
        out_shape=jax.ShapeDtypeStruct((B, S, D), BF16),
        scratch_shapes=[pltpu.VMEM((LANES, LANES), F32)],
        compiler_params=_params("arbitrary", "arbitrary", "arbitrary"),
        name="wkv7",
    )(r, lw, k, v, a, g, row(k_k), row(k_a), row(r_k), row(ln_w), row(ln_b))


def _proj_res_kernel(gated, *refs):
    if gated:
        a_ref, gate_ref, w_ref, res_ref, o_ref = refs
        a = (a_ref[...].astype(F32) * _sigmoid(gate_ref[...].astype(F32))).astype(BF16)
    else:
        a_ref, w_ref, res_ref, o_ref = refs
        a = a_ref[...]
    o_ref[...] = res_ref[...] + _dot(a, w_ref[...])


def _proj_res(a, w, res, gate=None, tm=512):
    n, kdim = a.shape
    d = w.shape[1]
    gated = gate is not None
    ins = [a] + ([gate] if gated else []) + [w.astype(BF16), res]
    specs = ([pl.BlockSpec((tm, kdim), lambda i: (i, 0))] * (2 if gated else 1)
             + [pl.BlockSpec((kdim, d), lambda i: (0, 0)), pl.BlockSpec((tm, d), lambda i: (i, 0))])
    return pl.pallas_call(
        functools.partial(_proj_res_kernel, gated),
        grid=(n // tm,),
        in_specs=specs,
        out_specs=pl.BlockSpec((tm, d), lambda i: (i, 0)),
        out_shape=jax.ShapeDtypeStruct((n, d), F32),
        compiler_params=_params("arbitrary"),
        name="proj_res_gated" if gated else "proj_res",
    )(*ins)


def _ffn_kernel(x_ref, gn_ref, w1_ref, w3_ref, w2_ref, o_ref, h_ref, acc_ref):
    f = pl.program_id(1)

    @pl.when(f == 0)
    def _():
        h_ref[...] = _rms(x_ref[...], gn_ref[...]).astype(BF16)

    h = h_ref[...]
    h1 = _dot(h, w1_ref[...])
    act = (h1 * _sigmoid(h1) * _dot(h, w3_ref[...])).astype(BF16)
    part = _dot(act, w2_ref[...])

    @pl.when(f == 0)
    def _():
        acc_ref[...] = part

    @pl.when(f > 0)
    def _():
        acc_ref[...] += part

    @pl.when(f == pl.num_programs(1) - 1)
    def _():
        o_ref[...] = x_ref[...] + acc_ref[...]


def _ffn(x, gn, w1, w3, w2, tm=512, tf=1408):
    n, d = x.shape
    fdim = w1.shape[1]
    return pl.pallas_call(
        _ffn_kernel,
        grid=(n // tm, fdim // tf),
        in_specs=[pl.BlockSpec((tm, d), lambda i, f: (i, 0)),
                  pl.BlockSpec((1, d), lambda i, f: (0, 0)),
                  pl.BlockSpec((d, tf), lambda i, f: (0, f)),
                  pl.BlockSpec((d, tf), lambda i, f: (0, f)),
                  pl.BlockSpec((tf, d), lambda i, f: (f, 0))],
        out_specs=pl.BlockSpec((tm, d), lambda i, f: (i, 0)),
        out_shape=jax.ShapeDtypeStruct((n, d), F32),
        scratch_shapes=[pltpu.VMEM((tm, d), BF16), pltpu.VMEM((tm, d), F32)],
        compiler_params=_params("arbitrary", "arbitrary"),
        name="ffn_dense",
    )(x, gn.reshape(1, d), w1.astype(BF16), w3.astype(BF16), w2.astype(BF16))


def _router_kernel(x_ref, gn_ref, wr_ref, h_ref, route_ref):
    h = _rms(x_ref[...], gn_ref[...])
    h_ref[...] = h
    logits = _dot_hi(h, wr_ref[...])
    lane = lax.broadcasted_iota(jnp.int32, logits.shape, 1)
    neg = jnp.float32(-jnp.inf)
    logits = jnp.where(lane < N_EXPERTS, logits, neg)
    m1 = jnp.max(logits, axis=-1, keepdims=True)
    i1 = jnp.min(jnp.where(logits == m1, lane, LANES), axis=-1, keepdims=True)
    rest = jnp.where(lane == i1, neg, logits)
    m2 = jnp.max(rest, axis=-1, keepdims=True)
    i2 = jnp.min(jnp.where(rest == m2, lane, LANES), axis=-1, keepdims=True)
    e2 = jnp.exp(m2 - m1)
    p1 = 1.0 / (1.0 + e2)
    p2 = e2 * p1
    route_ref[...] = jnp.where(lane == 0, i1.astype(F32),
                     jnp.where(lane == 1, i2.astype(F32),
                     jnp.where(lane == 2, p1, jnp.where(lane == 3, p2, 0.0))))


def _router(x, gn, w_router, tm=512):
    n, d = x.shape
    wr = jnp.zeros((d, LANES), F32).at[:, :N_EXPERTS].set(w_router)
    return pl.pallas_call(
        _router_kernel,
        grid=(n // tm,),
        in_specs=[pl.BlockSpec((tm, d), lambda i: (i, 0)),
                  pl.BlockSpec((1, d), lambda i: (0, 0)),
                  pl.BlockSpec((d, LANES), lambda i: (0, 0))],
        out_specs=[pl.BlockSpec((tm, d), lambda i: (i, 0)), pl.BlockSpec((tm, LANES), lambda i: (i, 0))],
        out_shape=[jax.ShapeDtypeStruct((n, d), F32), jax.ShapeDtypeStruct((n, LANES), F32)],
        compiler_params=_params("arbitrary"),
        name="moe_router",
    )(x, gn.reshape(1, d), wr)


def _gather_kernel(idx_ref, src_ref, o_ref, sem):
    rows = o_ref.shape[0]
    base = pl.program_id(0) * rows

    def row_copy(j):
        return pltpu.make_async_copy(src_ref.at[pl.ds(idx_ref[base + j], 1)], o_ref.at[pl.ds(j, 1)], sem)

    def start(j, c):
        row_copy(j).start()
        return c

    def wait(j, c):
        row_copy(j).wait()
        return c

    lax.fori_loop(0, rows, start, 0)
    lax.fori_loop(0, rows, wait, 0)


def _gather_rows(src, idx, rows=GATHER_ROWS):
    m = idx.shape[0]
    d = src.shape[1]
    return pl.pallas_call(
        _gather_kernel,
        grid_spec=pltpu.PrefetchScalarGridSpec(
            num_scalar_prefetch=1,
            grid=(m // rows,),
            in_specs=[pl.BlockSpec(memory_space=pl.ANY)],
            out_specs=pl.BlockSpec((rows, d), lambda i, idx: (i, 0)),
            scratch_shapes=[pltpu.SemaphoreType.DMA(())],
        ),
        out_shape=jax.ShapeDtypeStruct((m, d), src.dtype),
        compiler_params=_params("arbitrary"),
        name="gather_rows",
    )(idx, src)


def _experts_kernel(be_ref, nu_ref, x_ref, w1_ref, w3_ref, w2_ref, o_ref, xb_ref, acc_ref):
    i, f = pl.program_id(0), pl.program_id(1)
    used = i < nu_ref[0]

    @pl.when(jnp.logical_and(used, f == 0))
    def _():
        xb_ref[...] = x_ref[...].astype(BF16)

    @pl.when(used)
    def _():
        x = xb_ref[...]
        h1 = _dot(x, w1_ref[0])
        act = (h1 * _sigmoid(h1) * _dot(x, w3_ref[0])).astype(BF16)
        part = _dot(act, w2_ref[0])

        @pl.when(f == 0)
        def _():
            acc_ref[...] = part

        @pl.when(f > 0)
        def _():
            acc_ref[...] += part

    @pl.when(f == pl.num_programs(1) - 1)
    def _():
        o_ref[...] = jnp.where(used, acc_ref[...], 0.0)


def _experts(xs, blk_e, n_used, w1, w3, w2, tf=512):
    cap, d = xs.shape
    edim = w1.shape[2]
    nb = cap // MOE_BM
    return pl.pallas_call(
        _experts_kernel,
        grid_spec=pltpu.PrefetchScalarGridSpec(
            num_scalar_prefetch=2,
            grid=(nb, edim // tf),
            in_specs=[pl.BlockSpec((MOE_BM, d), lambda i, f, be, nu: (i, 0)),
                      pl.BlockSpec((1, d, tf), lambda i, f, be, nu: (be[i], 0, f)),
                      pl.BlockSpec((1, d, tf), lambda i, f, be, nu: (be[i], 0, f)),
                      pl.BlockSpec((1, tf, d), lambda i, f, be, nu: (be[i], f, 0))],
            out_specs=pl.BlockSpec((MOE_BM, d), lambda i, f, be, nu: (i, 0)),
            scratch_shapes=[pltpu.VMEM((MOE_BM, d), BF16), pltpu.VMEM((MOE_BM, d), F32)],
        ),
        out_shape=jax.ShapeDtypeStruct((cap, d), F32),
        compiler_params=_params("arbitrary", "arbitrary"),
        name="moe_experts",
    )(blk_e, n_used, xs, w1.astype(BF16), w3.astype(BF16), w2.astype(BF16))


def _combine_kernel(x_ref, route_ref, y0_ref, y1_ref, o_ref):
    p1 = route_ref[:, 2:3]
    p2 = route_ref[:, 3:4]
    o_ref[...] = x_ref[...] + (p1 * y0_ref[...] + p2 * y1_ref[...])


def _combine(x, route, yg, tm=512):
    n, d = x.shape
    nt = n // tm
    return pl.pallas_call(
        _combine_kernel,
        grid=(nt,),
        in_specs=[pl.BlockSpec((tm, d), lambda i: (i, 0)),
                  pl.BlockSpec((tm, LANES), lambda i: (i, 0)),
                  pl.BlockSpec((tm, d), lambda i: (i, 0)),
                  pl.BlockSpec((tm, d), lambda i: (i + nt, 0))],
        out_specs=pl.BlockSpec((tm, d), lambda i: (i, 0)),
        out_shape=jax.ShapeDtypeStruct((n, d), F32),
        compiler_params=_params("arbitrary"),
        name="moe_combine",
    )(x, route, yg, yg)


def _moe(x, gn, w_router, w1, w3, w2):
    n, d = x.shape
    h, route = _router(x, gn, w_router)
    e = route[:, :2].astype(jnp.int32).reshape(-1)
    onehot = (e[:, None] == jnp.arange(N_EXPERTS, dtype=jnp.int32)[None, :]).astype(jnp.int32)
    csum = jnp.cumsum(onehot, axis=0)
    rank = jnp.take_along_axis(csum, e[:, None], axis=1)[:, 0] - 1
    counts = csum[-1]
    padded = ((counts + MOE_BM - 1) // MOE_BM) * MOE_BM
    pend = jnp.cumsum(padded)
    dest = (pend - padded)[e] + rank
    nb = -(-(2 * n) // MOE_BM) + N_EXPERTS
    cap = nb * MOE_BM
    rows_tok = jnp.zeros((cap,), jnp.int32).at[dest].set(jnp.arange(2 * n, dtype=jnp.int32) // 2)
    blk_e = jnp.minimum(jnp.searchsorted(pend, jnp.arange(nb, dtype=jnp.int32) * MOE_BM, side="right"),
                        N_EXPERTS - 1).astype(jnp.int32)
    n_used = (pend[-1:] // MOE_BM).astype(jnp.int32)
    xs = _gather_rows(h, rows_tok)
    yb = _experts(xs, blk_e, n_used, w1, w3, w2)
    dest2 = dest.reshape(n, 2)
    yg = _gather_rows(yb, jnp.concatenate([dest2[:, 0], dest2[:, 1]]))
    return _combine(x, route, yg)


def _kv_kernel(x_ref, gn_ref, wk_ref, wv_ref, wf_ref, bf_ref, kn_ref, k_out, v_out, d_out, carry):
    s = pl.program_id(1)
    tm = x_ref.shape[1]

    @pl.when(s == 0)
    def _():
        carry[...] = jnp.zeros_like(carry)

    h = _rms(x_ref[0], gn_ref[...])
    hb = h.astype(BF16)
    k_out[0] = _head_rms(_dot(hb, wk_ref[...]), kn_ref[...]).astype(BF16)
    v_out[0] = _dot(hb, wv_ref[...]).astype(BF16)
    fl = _dot(hb, wf_ref[...]) + bf_ref[...]
    log_f = jnp.minimum(fl, 0.0) - jnp.log(1.0 + jnp.exp(-jnp.abs(fl)))
    tri = (lax.broadcasted_iota(jnp.int32, (tm, tm), 0) >= lax.broadcasted_iota(jnp.int32, (tm, tm), 1)).astype(F32)
    dcum = carry[...] + _dot_hi(tri, log_f)
    d_out[0] = dcum
    carry[...] = dcum[tm - 1:tm, :]


def _shared_kv(x, gn, w_kvf, b_f, k_norm, tm=256):
    B, S, D = x.shape
    wk = w_kvf[:, :D].astype(BF16)
    wv = w_kvf[:, D:2 * D].astype(BF16)
    wf = jnp.zeros((D, LANES), F32).at[:, :N_HEADS].set(w_kvf[:, 2 * D:]).astype(BF16)
    bf = jnp.zeros((1, LANES), F32).at[0, :N_HEADS].set(b_f)
    kn = jnp.tile(k_norm, N_HEADS).reshape(1, D)
    tok = pl.BlockSpec((1, tm, D), lambda b, s: (b, s, 0))
    full = lambda a: pl.BlockSpec(a.shape, lambda b, s: (0,) * a.ndim)
    ins = [x, gn.reshape(1, D), wk, wv, wf, bf, kn]
    return pl.pallas_call(
        _kv_kernel,
        grid=(B, S // tm),
        in_specs=[tok] + [full(a) for a in ins[1:]],
        out_specs=[tok, tok, pl.BlockSpec((1, tm, LANES), lambda b, s: (b, s, 0))],
        out_shape=[jax.ShapeDtypeStruct((B, S, D), BF16), jax.ShapeDtypeStruct((B, S, D), BF16),
                   jax.ShapeDtypeStruct((B, S, LANES), F32)],
        scratch_shapes=[pltpu.VMEM((1, LANES), F32)],
        compiler_params=_params("arbitrary", "arbitrary"),
        name="shared_kv",
    )(*ins)


def _qg_kernel(x_ref, gn_ref, wq_ref, wg_ref, qn_ref, q_out, og_out):
    hb = _rms(x_ref[...], gn_ref[...]).astype(BF16)
    q = _head_rms(_dot(hb, wq_ref[...]), qn_ref[...])
    q_out[...] = (q * (HEAD_DIM ** -0.5)).astype(BF16)
    og_out[...] = _dot(hb, wg_ref[...]).astype(BF16)


def _qg_proj(x, gn, w_qg, q_norm, tm=512):
    n, d = x.shape
    qn = jnp.tile(q_norm, N_HEADS).reshape(1, d)
    tok = pl.BlockSpec((tm, d), lambda i: (i, 0))
    const = lambda a: pl.BlockSpec(a.shape, lambda i: (0, 0))
    ins = [x, gn.reshape(1, d), w_qg[:, :d].astype(BF16), w_qg[:, d:].astype(BF16), qn]
    return pl.pallas_call(
        _qg_kernel,
        grid=(n // tm,),
        in_specs=[tok] + [const(a) for a in ins[1:]],
        out_specs=[tok, tok],
        out_shape=[jax.ShapeDtypeStruct((n, d), BF16)] * 2,
        compiler_params=_params("arbitrary"),
        name="fox_qg",
    )(*ins)


def _attn_kernel(q_ref, k_ref, v_ref, dt_ref, o_ref):
    tq, tk = ATT_TQ, ATT_TK
    i = pl.program_id(2)
    q = q_ref[0]
    lane = lax.broadcasted_iota(jnp.int32, (tq, LANES), 1)
    first = lane < HEAD_DIM
    zero = jnp.zeros_like(q)
    qh = (jnp.where(first, q, zero), jnp.where(first, zero, q))
    d0 = dt_ref[0, 0, :, pl.ds(pl.multiple_of(i * tq, tq), LANES)][:, 0:1]
    causal = (lax.broadcasted_iota(jnp.int32, (tq, tk), 0) >= lax.broadcasted_iota(jnp.int32, (tq, tk), 1))

    def step(j, carry, masked):
        sl = pl.ds(pl.multiple_of(j * tk, tk), tk)
        kt = k_ref[0, sl, :]
        vt = v_ref[0, sl, :]
        dk = dt_ref[0, 0, :, sl]
        out = []
        for hh in range(2):
            m, l, acc = carry[hh]
            s = _dot_nt(qh[hh], kt) + (d0[hh:hh + 1, :] - dk[hh:hh + 1, :])
            if masked:
                s = jnp.where(causal, s, -jnp.inf)
            m_new = jnp.maximum(m, jnp.max(s, axis=-1, keepdims=True))
            alpha = jnp.exp(m - m_new)
            p = jnp.exp(s - m_new)
            l = alpha * l + jnp.sum(p, axis=-1, keepdims=True)
            acc = alpha * acc + _dot(p.astype(BF16), vt)
            out.append((m_new, l, acc))
        return tuple(out)

    init = tuple((jnp.full((tq, 1), -jnp.inf, F32), jnp.zeros((tq, 1), F32), jnp.zeros((tq, LANES), F32))
                 for _ in range(2))
    carry = lax.fori_loop(0, i, lambda j, c: step(j, c, False), init)
    (_, l0, acc0), (_, l1, acc1) = step(i, carry, True)
    o_ref[0] = jnp.where(first, acc0 / l0, acc1 / l1).astype(BF16)


def _fox_attn(q, k, v, dt):
    B, S, D = q.shape
    assert ATT_TQ == ATT_TK
    return pl.pallas_call(
        _attn_kernel,
        grid=(B, HEAD_PAIRS, S // ATT_TQ),
        in_specs=[pl.BlockSpec((1, ATT_TQ, LANES), lambda b, p, i: (b, i, p)),
                  pl.BlockSpec((1, S, LANES), lambda b, p, i: (b, 0, p)),
                  pl.BlockSpec((1, S, LANES), lambda b, p, i: (b, 0, p)),
                  pl.BlockSpec((1, 1, 2, S), lambda b, p, i: (b, p, 0, 0))],
        out_specs=pl.BlockSpec((1, ATT_TQ, LANES), lambda b, p, i: (b, i, p)),
        out_shape=jax.ShapeDtypeStruct((B, S, D), BF16),
        compiler_params=_params("arbitrary", "arbitrary", "arbitrary"),
        name="fox_attn",
    )(q, k, v, dt)


def kernel(x, norm_mix, norm_ffn, rk_mu, rk_w_r, rk_w_k, rk_w_v, rk_w_o, rk_w0, rk_w1, rk_w2, rk_a0, rk_a1, rk_a2, rk_v0, rk_v1, rk_v2, rk_g1, rk_g2, rk_k_k, rk_k_a, rk_r_k, rk_lnx_w, rk_lnx_b, kv_norm, w_kvf, b_f, k_norm, fx_w_qg, fx_q_norm, fx_w_o, ffn_w1, ffn_w3, ffn_w2, moe_router, moe_w1, moe_w3, moe_w2):
    B, S, D = x.shape
    n = B * S
    depth = norm_mix.shape[0]
    n_rwkv = rk_mu.shape[0]
    v_first = None
    kv = None
    for l in range(depth):
        if l < n_rwkv:
            vres = None if l == 0 else (rk_v0[l - 1], rk_v1[l - 1], rk_v2[l - 1])
            r, lw, k, v, a, g = _rwkv_proj(
                x, norm_mix[l], rk_mu[l], rk_w_r[l], rk_w_k[l], rk_w_v[l], rk_w0[l], rk_w1[l], rk_w2[l],
                rk_a0[l], rk_a1[l], rk_a2[l], rk_g1[l], rk_g2[l], vres, v_first)
            if l == 0:
                v_first = v
            z = _wkv(r, lw, k, v, a, g, rk_k_k[l], rk_k_a[l], rk_r_k[l], rk_lnx_w[l], rk_lnx_b[l])
            x2 = _proj_res(z.reshape(n, D), rk_w_o[l], x.reshape(n, D))
        else:
            if kv is None:
                k_sh, v_sh, dcum = _shared_kv(x, kv_norm, w_kvf, b_f, k_norm)
                dt = dcum[:, :, :N_HEADS].transpose(0, 2, 1).reshape(B, HEAD_PAIRS, 2, S)
                kv = (k_sh, v_sh, dt)
            j = l - n_rwkv
            q, og = _qg_proj(x.reshape(n, D), norm_mix[l], fx_w_qg[j], fx_q_norm[j])
            o = _fox_attn(q.reshape(B, S, D), *kv)
            x2 = _proj_res(o.reshape(n, D), fx_w_o[j], x.reshape(n, D), gate=og)
        i = l // 2
        if l % 2 == 0:
            x2 = _ffn(x2, norm_ffn[l], ffn_w1[i], ffn_w3[i], ffn_w2[i])
        else:
            x2 = _moe(x2, norm_ffn[l], moe_router[i], moe_w1[i], moe_w3[i], moe_w2[i])
        x = x2.reshape(B, S, D)
    return x
```

```python
import functools

import jax
import jax.numpy as jnp
from jax import lax
from jax.experimental import pallas as pl
from jax.experimental.pallas import tpu as pltpu

F32 = jnp.float32
BF16 = jnp.bfloat16
HI = lax.Precision.HIGHEST

D_MODEL = 1024
HEAD_DIM = 64
N_HEADS = D_MODEL // HEAD_DIM
LANES = 128
HEAD_PAIRS = D_MODEL // LANES
N_EXPERTS = 8
RMS_EPS = 1e-6
GN_EPS = 64e-5
VMEM_LIMIT_BYTES = 56 * 1024 * 1024

WKV_CHUNK = 64
WKV_TILE = 512
ATT_TQ = 512
ATT_TK = 512
MOE_BM = 1024
GATHER_ROWS = 512


def _params(*sem):
    return pltpu.CompilerParams(dimension_semantics=sem, vmem_limit_bytes=VMEM_LIMIT_BYTES)


def _dot(a, b):
    return jnp.dot(a, b, preferred_element_type=F32)


def _dot_hi(a, b):
    return jnp.dot(a, b, preferred_element_type=F32, precision=HI)


def _dot_nt(a, b):
    return lax.dot_general(a, b, (((1,), (1,)), ((), ())), preferred_element_type=F32)


def _dot_tn(a, b):
    return lax.dot_general(a, b, (((0,), (0,)), ((), ())), preferred_element_type=F32)


def _rms(x, g):
    return x * lax.rsqrt(jnp.mean(x * x, axis=-1, keepdims=True) + RMS_EPS) * g


def _sigmoid(x):
    return 1.0 / (1.0 + jnp.exp(-x))


def _pair_ones():
    ri = lax.broadcasted_iota(jnp.int32, (LANES, LANES), 0)
    ci = lax.broadcasted_iota(jnp.int32, (LANES, LANES), 1)
    return ((ri < HEAD_DIM) == (ci < HEAD_DIM)).astype(F32)


def _head_rms(x, gain_row):
    ones = _pair_ones()
    parts = []
    for p in range(HEAD_PAIRS):
        xs = x[:, p * LANES:(p + 1) * LANES]
        ms = _dot_hi(xs * xs, ones) * (1.0 / HEAD_DIM)
        parts.append(xs * lax.rsqrt(ms + RMS_EPS))
    return jnp.concatenate(parts, axis=1) * gain_row


def _rwkv_proj_kernel(has_vres, *refs):
    if has_vres:
        (x_ref, gn_ref, mu_ref, wr_ref, wk_ref, wv_ref, w0_ref, w1_ref, w2_ref, a0_ref, a1_ref,
         a2_ref, g1_ref, g2_ref, v0_ref, v1_ref, v2_ref, vf_ref,
         r_out, lw_out, k_out, v_out, a_out, g_out, carry) = refs
    else:
        (x_ref, gn_ref, mu_ref, wr_ref, wk_ref, wv_ref, w0_ref, w1_ref, w2_ref, a0_ref, a1_ref,
         a2_ref, g1_ref, g2_ref,
         r_out, lw_out, k_out, v_out, a_out, g_out, carry) = refs
    s = pl.program_id(1)
    tm = x_ref.shape[1]

    @pl.when(s == 0)
    def _():
        carry[...] = jnp.zeros_like(carry)

    h = _rms(x_ref[0], gn_ref[...])
    prev_last = carry[...]
    carry[...] = h[tm - 1:tm, :]
    row = lax.broadcasted_iota(jnp.int32, (tm, 1), 0)
    xx = jnp.where(row == 0, prev_last, pltpu.roll(h, 1, 0)) - h

    def mix(i):
        return (h + xx * mu_ref[i:i + 1, :]).astype(BF16)

    xr, xw, xk, xv, xa, xg = [mix(i) for i in range(6)]
    r_out[0] = _dot(xr, wr_ref[...]).astype(BF16)
    k_out[0] = _dot(xk, wk_ref[...]).astype(BF16)
    v = _dot(xv, wv_ref[...])
    wl = w0_ref[...] + _dot(jnp.tanh(_dot(xw, w1_ref[...])).astype(BF16), w2_ref[...])
    w = -(jnp.maximum(-wl, 0.0) + jnp.log(1.0 + jnp.exp(-jnp.abs(wl)))) - 0.5
    lw_out[0] = -jnp.exp(w)
    a_out[0] = _sigmoid(a0_ref[...] + _dot(_dot(xa, a1_ref[...]).astype(BF16), a2_ref[...])).astype(BF16)
    g_out[0] = _dot(_sigmoid(_dot(xg, g1_ref[...])).astype(BF16), g2_ref[...]).astype(BF16)
    if has_vres:
        gate = _sigmoid(v0_ref[...] + _dot(_dot(xv, v1_ref[...]).astype(BF16), v2_ref[...]))
        v = v + (vf_ref[0].astype(F32) - v) * gate
    v_out[0] = v.astype(BF16)


def _rwkv_proj(x, gn, mu, wr, wk, wv, w0, w1, w2, a0, a1, a2, g1, g2, vres, v_first, tm=256):
    B, S, D = x.shape
    has_vres = vres is not None
    row = lambda a: a.reshape(1, -1)
    full = lambda a: pl.BlockSpec(a.shape, lambda b, s: (0,) * a.ndim)
    tok = pl.BlockSpec((1, tm, D), lambda b, s: (b, s, 0))
    ins = [x, row(gn), mu, wr.astype(BF16), wk.astype(BF16), wv.astype(BF16), row(w0),
           w1.astype(BF16), w2.astype(BF16), row(a0), a1.astype(BF16), a2.astype(BF16),
           g1.astype(BF16), g2.astype(BF16)]
    if has_vres:
        v0, v1, v2 = vres
        ins += [row(v0), v1.astype(BF16), v2.astype(BF16)]
    specs = [tok] + [full(a) for a in ins[1:]]
    if has_vres:
        ins.append(v_first)
        specs.append(tok)
    out_shape = [jax.ShapeDtypeStruct((B, S, D), dt) for dt in (BF16, F32, BF16, BF16, BF16, BF16)]
    return pl.pallas_call(
        functools.partial(_rwkv_proj_kernel, has_vres),
        grid=(B, S // tm),
        in_specs=specs,
        out_specs=[tok] * 6,
        out_shape=out_shape,
        scratch_shapes=[pltpu.VMEM((1, D), F32)],
        compiler_params=_params("arbitrary", "arbitrary"),
        name="rwkv_proj_vres" if has_vres else "rwkv_proj",
    )(*ins)


def _head_sum(x, ones_b):
    hi = x.astype(BF16)
    lo = (x - hi.astype(F32)).astype(BF16)
    return _dot(hi, ones_b) + _dot(lo, ones_b)


def _wkv_kernel(r_ref, lw_ref, k_ref, v_ref, a_ref, g_ref, kk_ref, ka_ref, rk_ref, lnw_ref, lnb_ref,
                z_ref, state):
    C = WKV_CHUNK
    P = 2 * C
    tt = r_ref.shape[1]
    nc = tt // C

    @pl.when(pl.program_id(2) == 0)
    def _():
        state[...] = jnp.zeros_like(state)

    lane = lax.broadcasted_iota(jnp.int32, (C, LANES), 1)
    m0 = (lane < HEAD_DIM).astype(F32)
    m1 = 1.0 - m0
    ri = lax.broadcasted_iota(jnp.int32, (P, P), 0)
    ci = lax.broadcasted_iota(jnp.int32, (P, P), 1)
    strict = ri > ci
    incl = ri >= ci
    eye = (ri == ci).astype(F32)
    tri = (lax.broadcasted_iota(jnp.int32, (C, C), 0) >= lax.broadcasted_iota(jnp.int32, (C, C), 1)).astype(F32)
    ones_b = _pair_ones().astype(BF16)

    def stack(x):
        return jnp.concatenate([x * m0, x * m1], axis=0).astype(BF16)

    r = r_ref[0].astype(F32)
    k = k_ref[0].astype(F32)
    v = v_ref[0].astype(F32)
    a = a_ref[0].astype(F32)
    lw = lw_ref[0]
    kk = k * kk_ref[...]
    kk = kk * lax.rsqrt(jnp.maximum(_head_sum(kk * kk, ones_b), 1e-24))
    k2 = k * (1.0 + (a - 1.0) * ka_ref[...])
    bv = kk * a
    cum_c = _dot_hi(tri, jnp.concatenate([lw[c * C:(c + 1) * C] for c in range(nc)], axis=1))
    cum = jnp.concatenate([cum_c[:, c * LANES:(c + 1) * LANES] for c in range(nc)], axis=0)
    cum_end = [cum_c[C - 1:C, c * LANES:(c + 1) * LANES] for c in range(nc)]
    to_end = jnp.concatenate([jnp.broadcast_to(ce, (C, LANES)) for ce in cum_end], axis=0) - cum
    g_inv = jnp.exp(-cum)
    g_end = jnp.exp(to_end)
    a_t = -kk * jnp.exp(cum - lw)
    r_t = r * jnp.exp(cum)
    b_t = bv * g_inv
    k_t = k2 * g_inv
    b_h = bv * g_end
    k_h = k2 * g_end

    cs = range(nc)

    def stacks(x):
        return [stack(x[c * C:(c + 1) * C]) for c in cs]

    a_s, r_s, b_s, k_s, v_s, bh_s, kh_s = (stacks(x) for x in (a_t, r_t, b_t, k_t, v, b_h, k_h))
    n_ab = [jnp.where(strict, _dot_nt(a_s[c], b_s[c]), 0.0) for c in cs]
    a_ak = [jnp.where(strict, _dot_nt(a_s[c], k_s[c]), 0.0).astype(BF16) for c in cs]
    a_rb = [jnp.where(incl, _dot_nt(r_s[c], b_s[c]), 0.0).astype(BF16) for c in cs]
    a_rk = [jnp.where(incl, _dot_nt(r_s[c], k_s[c]), 0.0).astype(BF16) for c in cs]
    inv = [eye + n for n in n_ab]
    pw = [n.astype(BF16) for n in n_ab]
    for _ in range(5):
        pw = [_dot(p, p).astype(BF16) for p in pw]
        inv = [t + _dot(t.astype(BF16), p) for t, p in zip(inv, pw)]
    inv = [t.astype(BF16) for t in inv]
    w = [_dot(inv[c], a_s[c]).astype(BF16) for c in cs]
    akv = [_dot(a_ak[c], v_s[c]).astype(BF16) for c in cs]
    u0 = [_dot(inv[c], akv[c]).astype(BF16) for c in cs]
    m_st = [_dot_tn(w[c], bh_s[c]).astype(BF16) for c in cs]
    z_st = [_dot_tn(u0[c], bh_s[c]) + _dot_tn(v_s[c], kh_s[c]) for c in cs]
    q_h = [(r_s[c].astype(F32) + _dot(a_rb[c], w[c])).astype(BF16) for c in cs]
    y0 = [_dot(a_rb[c], u0[c]) + _dot(a_rk[c], v_s[c]) for c in cs]

    s = state[...]
    ys = []
    for c in cs:
        sb = s.astype(BF16)
        y_s = _dot_nt(q_h[c], sb) + y0[c]
        ys.append(y_s[:C] + y_s[C:])
        s = s * jnp.exp(cum_end[c]) + _dot(sb, m_st[c]) + z_st[c]
    state[...] = s

    y = jnp.concatenate(ys, axis=0)
    mean = _head_sum(y, ones_b) * (1.0 / HEAD_DIM)
    yc = y - mean
    var = _head_sum(yc * yc, ones_b) * (1.0 / HEAD_DIM)
    yn = yc * lax.rsqrt(var + GN_EPS) * lnw_ref[...] + lnb_ref[...]
    bonus = _head_sum(r * k2 * rk_ref[...], ones_b) * v
    z_ref[0] = ((yn + bonus) * g_ref[0].astype(F32)).astype(BF16)


def _wkv(r, lw, k, v, a, g, k_k, k_a, r_k, ln_w, ln_b):
    B, S, D = r.shape
    tt = min(WKV_TILE, S)
    tok = pl.BlockSpec((1, tt, LANES), lambda b, p, t: (b, t, p))
    par = pl.BlockSpec((1, LANES), lambda b, p, t: (0, p))
    row = lambda x: x.reshape(1, D)
    return pl.pallas_call(
        _wkv_kernel,
        grid=(B, HEAD_PAIRS, S // tt),
        in_specs=[tok] * 6 + [par] * 5,
        out_specs=tok,
        out_shape=jax.ShapeDtypeStruct((B, S, D), BF16),
        scratch_shapes=[pltpu.VMEM((LANES, LANES), F32)],
        compiler_params=_params("arbitrary", "arbitrary", "arbitrary"),
        name="wkv7",
    )(r, lw, k, v, a, g, row(k_k), row(k_a), row(r_k), row(ln_w), row(ln_b))


def _proj_res_kernel(gated, *refs):
    if gated:
        a_ref, gate_ref, w_ref, res_ref, o_ref = refs
        a = (a_ref[...].astype(F32) * _sigmoid(gate_ref[...].astype(F32))).astype(BF16)
    else:
        a_ref, w_ref, res_ref, o_ref = refs
        a = a_ref[...]
    o_ref[...] = res_ref[...] + _dot(a, w_ref[...])


def _proj_res(a, w, res, gate=None, tm=512):
    n, kdim = a.shape
    d = w.shape[1]
    gated = gate is not None
    ins = [a] + ([gate] if gated else []) + [w.astype(BF16), res]
    specs = ([pl.BlockSpec((tm, kdim), lambda i: (i, 0))] * (2 if gated else 1)
             + [pl.BlockSpec((kdim, d), lambda i: (0, 0)), pl.BlockSpec((tm, d), lambda i: (i, 0))])
    return pl.pallas_call(
        functools.partial(_proj_res_kernel, gated),
        grid=(n // tm,),
        in_specs=specs,
        out_specs=pl.BlockSpec((tm, d), lambda i: (i, 0)),
        out_shape=jax.ShapeDtypeStruct((n, d), F32),
        compiler_params=_params("arbitrary"),
        name="proj_res_gated" if gated else "proj_res",
    )(*ins)


def _ffn_kernel(x_ref, gn_ref, w1_ref, w3_ref, w2_ref, o_ref, h_ref, acc_ref):
    f = pl.program_id(1)

    @pl.when(f == 0)
    def _():
        h_ref[...] = _rms(x_ref[...], gn_ref[...]).astype(BF16)

    h = h_ref[...]
    h1 = _dot(h, w1_ref[...])
    act = (h1 * _sigmoid(h1) * _dot(h, w3_ref[...])).astype(BF16)
    part = _dot(act, w2_ref[...])

    @pl.when(f == 0)
    def _():
        acc_ref[...] = part

    @pl.when(f > 0)
    def _():
        acc_ref[...] += part

    @pl.when(f == pl.num_programs(1) - 1)
    def _():
        o_ref[...] = x_ref[...] + acc_ref[...]


def _ffn(x, gn, w1, w3, w2, tm=512, tf=1408):
    n, d = x.shape
    fdim = w1.shape[1]
    return pl.pallas_call(
        _ffn_kernel,
        grid=(n // tm, fdim // tf),
        in_specs=[pl.BlockSpec((tm, d), lambda i, f: (i, 0)),
                  pl.BlockSpec((1, d), lambda i, f: (0, 0)),
                  pl.BlockSpec((d, tf), lambda i, f: (0, f)),
                  pl.BlockSpec((d, tf), lambda i, f: (0, f)),
                  pl.BlockSpec((tf, d), lambda i, f: (f, 0))],
        out_specs=pl.BlockSpec((tm, d), lambda i, f: (i, 0)),
        out_shape=jax.ShapeDtypeStruct((n, d), F32),
        scratch_shapes=[pltpu.VMEM((tm, d), BF16), pltpu.VMEM((tm, d), F32)],
        compiler_params=_params("arbitrary", "arbitrary"),
        name="ffn_dense",
    )(x, gn.reshape(1, d), w1.astype(BF16), w3.astype(BF16), w2.astype(BF16))


def _router_kernel(x_ref, gn_ref, wr_ref, h_ref, route_ref):
    h = _rms(x_ref[...], gn_ref[...])
    h_ref[...] = h
    logits = _dot_hi(h, wr_ref[...])
    lane = lax.broadcasted_iota(jnp.int32, logits.shape, 1)
    neg = jnp.float32(-jnp.inf)
    logits = jnp.where(lane < N_EXPERTS, logits, neg)
    m1 = jnp.max(logits, axis=-1, keepdims=True)
    i1 = jnp.min(jnp.where(logits == m1, lane, LANES), axis=-1, keepdims=True)
    rest = jnp.where(lane == i1, neg, logits)
    m2 = jnp.max(rest, axis=-1, keepdims=True)
    i2 = jnp.min(jnp.where(rest == m2, lane, LANES), axis=-1, keepdims=True)
    e2 = jnp.exp(m2 - m1)
    p1 = 1.0 / (1.0 + e2)
    p2 = e2 * p1
    route_ref[...] = jnp.where(lane == 0, i1.astype(F32),
                     jnp.where(lane == 1, i2.astype(F32),
                     jnp.where(lane == 2, p1, jnp.where(lane == 3, p2, 0.0))))


def _router(x, gn, w_router, tm=512):
    n, d = x.shape
    wr = jnp.zeros((d, LANES), F32).at[:, :N_EXPERTS].set(w_router)
    return pl.pallas_call(
        _router_kernel,
        grid=(n // tm,),
        in_specs=[pl.BlockSpec((tm, d), lambda i: (i, 0)),
                  pl.BlockSpec((1, d), lambda i: (0, 0)),
                  pl.BlockSpec((d, LANES), lambda i: (0, 0))],
        out_specs=[pl.BlockSpec((tm, d), lambda i: (i, 0)), pl.BlockSpec((tm, LANES), lambda i: (i, 0))],
        out_shape=[jax.ShapeDtypeStruct((n, d), F32), jax.ShapeDtypeStruct((n, LANES), F32)],
        compiler_params=_params("arbitrary"),
        name="moe_router",
    )(x, gn.reshape(1, d), wr)


def _gather_kernel(idx_ref, src_ref, o_ref, sem):
    rows = o_ref.shape[0]
    base = pl.program_id(0) * rows

    def row_copy(j):
        return pltpu.make_async_copy(src_ref.at[pl.ds(idx_ref[base + j], 1)], o_ref.at[pl.ds(j, 1)], sem)

    def start(j, c):
        row_copy(j).start()
        return c

    def wait(j, c):
        row_copy(j).wait()
        return c

    lax.fori_loop(0, rows, start, 0, unroll=8)
    lax.fori_loop(0, rows, wait, 0, unroll=8)


def _gather_rows(src, idx, rows=GATHER_ROWS):
    m = idx.shape[0]
    d = src.shape[1]
    return pl.pallas_call(
        _gather_kernel,
        grid_spec=pltpu.PrefetchScalarGridSpec(
            num_scalar_prefetch=1,
            grid=(m // rows,),
            in_specs=[pl.BlockSpec(memory_space=pl.ANY)],
            out_specs=pl.BlockSpec((rows, d), lambda i, idx: (i, 0)),
            scratch_shapes=[pltpu.SemaphoreType.DMA(())],
        ),
        out_shape=jax.ShapeDtypeStruct((m, d), src.dtype),
        compiler_params=_params("arbitrary"),
        name="gather_rows",
    )(idx, src)


def _experts_kernel(be_ref, nu_ref, x_ref, w1_ref, w3_ref, w2_ref, o_ref, xb_ref, acc_ref):
    i, f = pl.program_id(0), pl.program_id(1)
    used = i < nu_ref[0]

    @pl.when(jnp.logical_and(used, f == 0))
    def _():
        xb_ref[...] = x_ref[...].astype(BF16)

    @pl.when(used)
    def _():
        x = xb_ref[...]
        h1 = _dot(x, w1_ref[0])
        act = (h1 * _sigmoid(h1) * _dot(x, w3_ref[0])).astype(BF16)
        part = _dot(act, w2_ref[0])

        @pl.when(f == 0)
        def _():
            acc_ref[...] = part

        @pl.when(f > 0)
        def _():
            acc_ref[...] += part

    @pl.when(f == pl.num_programs(1) - 1)
    def _():
        o_ref[...] = jnp.where(used, acc_ref[...], 0.0)


def _experts(xs, blk_e, n_used, w1, w3, w2, tf=512):
    cap, d = xs.shape
    edim = w1.shape[2]
    nb = cap // MOE_BM
    return pl.pallas_call(
        _experts_kernel,
        grid_spec=pltpu.PrefetchScalarGridSpec(
            num_scalar_prefetch=2,
            grid=(nb, edim // tf),
            in_specs=[pl.BlockSpec((MOE_BM, d), lambda i, f, be, nu: (i, 0)),
                      pl.BlockSpec((1, d, tf), lambda i, f, be, nu: (be[i], 0, f)),
                      pl.BlockSpec((1, d, tf), lambda i, f, be, nu: (be[i], 0, f)),
                      pl.BlockSpec((1, tf, d), lambda i, f, be, nu: (be[i], f, 0))],
            out_specs=pl.BlockSpec((MOE_BM, d), lambda i, f, be, nu: (i, 0)),
            scratch_shapes=[pltpu.VMEM((MOE_BM, d), BF16), pltpu.VMEM((MOE_BM, d), F32)],
        ),
        out_shape=jax.ShapeDtypeStruct((cap, d), F32),
        compiler_params=_params("arbitrary", "arbitrary"),
        name="moe_experts",
    )(blk_e, n_used, xs, w1.astype(BF16), w3.astype(BF16), w2.astype(BF16))


def _combine_kernel(x_ref, route_ref, y0_ref, y1_ref, o_ref):
    p1 = route_ref[:, 2:3]
    p2 = route_ref[:, 3:4]
    o_ref[...] = x_ref[...] + (p1 * y0_ref[...] + p2 * y1_ref[...])


def _combine(x, route, yg, tm=512):
    n, d = x.shape
    nt = n // tm
    return pl.pallas_call(
        _combine_kernel,
        grid=(nt,),
        in_specs=[pl.BlockSpec((tm, d), lambda i: (i, 0)),
                  pl.BlockSpec((tm, LANES), lambda i: (i, 0)),
                  pl.BlockSpec((tm, d), lambda i: (i, 0)),
                  pl.BlockSpec((tm, d), lambda i: (i + nt, 0))],
        out_specs=pl.BlockSpec((tm, d), lambda i: (i, 0)),
        out_shape=jax.ShapeDtypeStruct((n, d), F32),
        compiler_params=_params("arbitrary"),
        name="moe_combine",
    )(x, route, yg, yg)


def _moe(x, gn, w_router, w1, w3, w2):
    n, d = x.shape
    h, route = _router(x, gn, w_router)
    e = route[:, :2].astype(jnp.int32).reshape(-1)
    onehot = (e[:, None] == jnp.arange(N_EXPERTS, dtype=jnp.int32)[None, :]).astype(jnp.int32)
    csum = jnp.cumsum(onehot, axis=0)
    rank = jnp.take_along_axis(csum, e[:, None], axis=1)[:, 0] - 1
    counts = csum[-1]
    padded = ((counts + MOE_BM - 1) // MOE_BM) * MOE_BM
    pend = jnp.cumsum(padded)
    dest = (pend - padded)[e] + rank
    nb = -(-(2 * n) // MOE_BM) + N_EXPERTS
    cap = nb * MOE_BM
    rows_tok = jnp.zeros((cap,), jnp.int32).at[dest].set(jnp.arange(2 * n, dtype=jnp.int32) // 2)
    blk_e = jnp.minimum(jnp.searchsorted(pend, jnp.arange(nb, dtype=jnp.int32) * MOE_BM, side="right"),
                        N_EXPERTS - 1).astype(jnp.int32)
    n_used = (pend[-1:] // MOE_BM).astype(jnp.int32)
    xs = _gather_rows(h, rows_tok)
    yb = _experts(xs, blk_e, n_used, w1, w3, w2)
    dest2 = dest.reshape(n, 2)
    yg = _gather_rows(yb, jnp.concatenate([dest2[:, 0], dest2[:, 1]]))
    return _combine(x, route, yg)


def _kv_kernel(x_ref, gn_ref, wk_ref, wv_ref, wf_ref, bf_ref, kn_ref, k_out, v_out, dp_out, carry):
    s = pl.program_id(1)
    tm = x_ref.shape[1]

    @pl.when(s == 0)
    def _():
        carry[...] = jnp.zeros_like(carry)

    h = _rms(x_ref[0], gn_ref[...])
    hb = h.astype(BF16)
    k_out[0] = _head_rms(_dot(hb, wk_ref[...]), kn_ref[...]).astype(BF16)
    v_out[0] = _dot(hb, wv_ref[...]).astype(BF16)
    fl = _dot(hb, wf_ref[...]) + bf_ref[...]
    log_f = jnp.minimum(fl, 0.0) - jnp.log(1.0 + jnp.exp(-jnp.abs(fl)))
    tri = (lax.broadcasted_iota(jnp.int32, (tm, tm), 0) >= lax.broadcasted_iota(jnp.int32, (tm, tm), 1)).astype(F32)
    dcum = carry[...] + _dot_hi(tri, log_f)
    carry[...] = dcum[tm - 1:tm, :]
    rest = -dcum
    pieces = []
    for _ in range(N_BIAS):
        piece = rest.astype(BF16)
        pieces.append(piece)
        rest = rest - piece.astype(F32)
    dp_out[0] = jnp.concatenate(pieces, axis=1)


def _shared_kv(x, gn, w_kvf, b_f, k_norm, tm=256):
    B, S, D = x.shape
    wk = w_kvf[:, :D].astype(BF16)
    wv = w_kvf[:, D:2 * D].astype(BF16)
    wf = jnp.zeros((D, LANES), F32).at[:, :N_HEADS].set(w_kvf[:, 2 * D:]).astype(BF16)
    bf = jnp.zeros((1, LANES), F32).at[0, :N_HEADS].set(b_f)
    kn = jnp.tile(k_norm, N_HEADS).reshape(1, D)
    tok = pl.BlockSpec((1, tm, D), lambda b, s: (b, s, 0))
    full = lambda a: pl.BlockSpec(a.shape, lambda b, s: (0,) * a.ndim)
    ins = [x, gn.reshape(1, D), wk, wv, wf, bf, kn]
    return pl.pallas_call(
        _kv_kernel,
        grid=(B, S // tm),
        in_specs=[tok] + [full(a) for a in ins[1:]],
        out_specs=[tok, tok, pl.BlockSpec((1, tm, N_BIAS * LANES), lambda b, s: (b, s, 0))],
        out_shape=[jax.ShapeDtypeStruct((B, S, D), BF16), jax.ShapeDtypeStruct((B, S, D), BF16),
                   jax.ShapeDtypeStruct((B, S, N_BIAS * LANES), BF16)],
        scratch_shapes=[pltpu.VMEM((1, LANES), F32)],
        compiler_params=_params("arbitrary", "arbitrary"),
        name="shared_kv",
    )(*ins)


def _qg_kernel(x_ref, gn_ref, wq_ref, wg_ref, qn_ref, q_out, og_out):
    hb = _rms(x_ref[...], gn_ref[...]).astype(BF16)
    q = _head_rms(_dot(hb, wq_ref[...]), qn_ref[...])
    q_out[...] = (q * (HEAD_DIM ** -0.5)).astype(BF16)
    og_out[...] = _dot(hb, wg_ref[...]).astype(BF16)


def _qg_proj(x, gn, w_qg, q_norm, tm=512):
    n, d = x.shape
    qn = jnp.tile(q_norm, N_HEADS).reshape(1, d)
    tok = pl.BlockSpec((tm, d), lambda i: (i, 0))
    const = lambda a: pl.BlockSpec(a.shape, lambda i: (0, 0))
    ins = [x, gn.reshape(1, d), w_qg[:, :d].astype(BF16), w_qg[:, d:].astype(BF16), qn]
    return pl.pallas_call(
        _qg_kernel,
        grid=(n // tm,),
        in_specs=[tok] + [const(a) for a in ins[1:]],
        out_specs=[tok, tok],
        out_shape=[jax.ShapeDtypeStruct((n, d), BF16)] * 2,
        compiler_params=_params("arbitrary"),
        name="fox_qg",
    )(*ins)


N_BIAS = 3


def _attn_kernel(qt_ref, k_ref, vt_ref, o_ref):
    tq, tk = ATT_TQ, ATT_TK
    i = pl.program_id(2)
    qt = qt_ref[0, 0]
    q_pos = i * tq + lax.broadcasted_iota(jnp.int32, (tk, tq), 1)
    k_off = lax.broadcasted_iota(jnp.int32, (tk, tq), 0)

    def step(j, carry, masked):
        m, l, acc = carry
        off = pl.multiple_of(j * tk, tk)
        s = _dot(k_ref[0, 0, pl.ds(off, tk), :], qt)
        if masked:
            s = jnp.where(k_off + off <= q_pos, s, -jnp.inf)
        m_new = jnp.maximum(m, jnp.max(s, axis=0, keepdims=True))
        alpha = jnp.exp(m - m_new)
        p = jnp.exp(s - m_new)
        l = alpha * l + jnp.sum(p, axis=0, keepdims=True)
        acc = alpha * acc + _dot(vt_ref[0, 0, :, pl.ds(off, tk)], p.astype(BF16))
        return m_new, l, acc

    init = (jnp.full((1, tq), -jnp.inf, F32), jnp.zeros((1, tq), F32), jnp.zeros((HEAD_DIM, tq), F32))
    carry = lax.fori_loop(0, i, lambda j, c: step(j, c, False), init)
    _, l, acc = step(i, carry, True)
    o_ref[0, 0] = (acc / l).astype(BF16)


def _fox_attn(qt, k_aug, vt):
    B, H, _, S = qt.shape
    return pl.pallas_call(
        _attn_kernel,
        grid=(B, H, S // ATT_TQ),
        in_specs=[pl.BlockSpec((1, 1, LANES, ATT_TQ), lambda b, h, i: (b, h, 0, i)),
                  pl.BlockSpec((1, 1, S, LANES), lambda b, h, i: (b, h, 0, 0)),
                  pl.BlockSpec((1, 1, HEAD_DIM, S), lambda b, h, i: (b, h, 0, 0))],
        out_specs=pl.BlockSpec((1, 1, HEAD_DIM, ATT_TQ), lambda b, h, i: (b, h, 0, i)),
        out_shape=jax.ShapeDtypeStruct((B, H, HEAD_DIM, S), BF16),
        compiler_params=_params("arbitrary", "arbitrary", "arbitrary"),
        name="fox_attn",
    )(qt, k_aug, vt)


def _split_heads_t(a, B, S):
    return a.reshape(B, S, N_HEADS, HEAD_DIM).transpose(0, 2, 3, 1)


def _attn_kv_layout(k_sh, v_sh, d_pieces):
    B, S, _ = k_sh.shape
    kh = k_sh.reshape(B, S, N_HEADS, HEAD_DIM).transpose(0, 2, 1, 3)
    dp = d_pieces.reshape(B, S, N_BIAS, LANES)[..., :N_HEADS].transpose(0, 3, 1, 2)
    pad = jnp.zeros((B, N_HEADS, S, LANES - HEAD_DIM - N_BIAS), BF16)
    return jnp.concatenate([kh, dp, pad], axis=-1), _split_heads_t(v_sh, B, S)


def _attn_q_layout(q, B, S):
    qt = _split_heads_t(q, B, S)
    ones = jnp.ones((B, N_HEADS, N_BIAS, S), BF16)
    pad = jnp.zeros((B, N_HEADS, LANES - HEAD_DIM - N_BIAS, S), BF16)
    return jnp.concatenate([qt, ones, pad], axis=2)


def kernel(x, norm_mix, norm_ffn, rk_mu, rk_w_r, rk_w_k, rk_w_v, rk_w_o, rk_w0, rk_w1, rk_w2, rk_a0, rk_a1, rk_a2, rk_v0, rk_v1, rk_v2, rk_g1, rk_g2, rk_k_k, rk_k_a, rk_r_k, rk_lnx_w, rk_lnx_b, kv_norm, w_kvf, b_f, k_norm, fx_w_qg, fx_q_norm, fx_w_o, ffn_w1, ffn_w3, ffn_w2, moe_router, moe_w1, moe_w3, moe_w2):
    B, S, D = x.shape
    n = B * S
    depth = norm_mix.shape[0]
    n_rwkv = rk_mu.shape[0]
    v_first = None
    kv = None
    for l in range(depth):
        if l < n_rwkv:
            vres = None if l == 0 else (rk_v0[l - 1], rk_v1[l - 1], rk_v2[l - 1])
            r, lw, k, v, a, g = _rwkv_proj(
                x, norm_mix[l], rk_mu[l], rk_w_r[l], rk_w_k[l], rk_w_v[l], rk_w0[l], rk_w1[l], rk_w2[l],
                rk_a0[l], rk_a1[l], rk_a2[l], rk_g1[l], rk_g2[l], vres, v_first)
            if l == 0:
                v_first = v
            z = _wkv(r, lw, k, v, a, g, rk_k_k[l], rk_k_a[l], rk_r_k[l], rk_lnx_w[l], rk_lnx_b[l])
            x2 = _proj_res(z.reshape(n, D), rk_w_o[l], x.reshape(n, D))
        else:
            if kv is None:
                k_sh, v_sh, d_pieces = _shared_kv(x, kv_norm, w_kvf, b_f, k_norm)
                kv = _attn_kv_layout(k_sh, v_sh, d_pieces)
            j = l - n_rwkv
            q, og = _qg_proj(x.reshape(n, D), norm_mix[l], fx_w_qg[j], fx_q_norm[j])
            ot = _fox_attn(_attn_q_layout(q, B, S), *kv)
            o = ot.transpose(0, 3, 1, 2).reshape(n, D)
            x2 = _proj_res(o, fx_w_o[j], x.reshape(n, D), gate=og)
        i = l // 2
        if l % 2 == 0:
            x2 = _ffn(x2, norm_ffn[l], ffn_w1[i], ffn_w3[i], ffn_w2[i])
        else:
            x2 = _moe(x2, norm_ffn[l], moe_router[i], moe_w1[i], moe_w3[i], moe_w2[i])
        x = x2.reshape(B, S, D)
    return x
```

```python
import functools

import jax
import jax.numpy as jnp
from jax import lax
from jax.experimental import pallas as pl
from jax.experimental.pallas import tpu as pltpu

F32 = jnp.float32
BF16 = jnp.bfloat16
HI = lax.Precision.HIGHEST

D_MODEL = 1024
HEAD_DIM = 64
N_HEADS = D_MODEL // HEAD_DIM
LANES = 128
HEAD_PAIRS = D_MODEL // LANES
N_EXPERTS = 8
LOG2_E = 1.4426950408889634
RMS_EPS = 1e-6
GN_EPS = 64e-5
VMEM_LIMIT_BYTES = 56 * 1024 * 1024

WKV_CHUNK = 64
WKV_TILE = 512
WKV_PAIRS = 2
ATT_TQ = 512
ATT_TK = 512
ATT_HEADS = 2
MOE_BM = 1024
GATHER_ROWS = 512


def _params(*sem):
    return pltpu.CompilerParams(dimension_semantics=sem, vmem_limit_bytes=VMEM_LIMIT_BYTES)


def _dot(a, b):
    return jnp.dot(a, b, preferred_element_type=F32)


def _dot_hi(a, b):
    return jnp.dot(a, b, preferred_element_type=F32, precision=HI)


def _dot_nt(a, b):
    return lax.dot_general(a, b, (((1,), (1,)), ((), ())), preferred_element_type=F32)


def _dot_tn(a, b):
    return lax.dot_general(a, b, (((0,), (0,)), ((), ())), preferred_element_type=F32)


def _rms(x, g):
    return x * lax.rsqrt(jnp.mean(x * x, axis=-1, keepdims=True) + RMS_EPS) * g


def _sigmoid(x):
    return 1.0 / (1.0 + jnp.exp(-x))


def _pair_ones():
    ri = lax.broadcasted_iota(jnp.int32, (LANES, LANES), 0)
    ci = lax.broadcasted_iota(jnp.int32, (LANES, LANES), 1)
    return ((ri < HEAD_DIM) == (ci < HEAD_DIM)).astype(F32)


def _head_rms(x, gain_row):
    ones = _pair_ones()
    parts = []
    for p in range(HEAD_PAIRS):
        xs = x[:, p * LANES:(p + 1) * LANES]
        ms = _dot_hi(xs * xs, ones) * (1.0 / HEAD_DIM)
        parts.append(xs * lax.rsqrt(ms + RMS_EPS))
    return jnp.concatenate(parts, axis=1) * gain_row


def _rwkv_proj_kernel(has_vres, *refs):
    if has_vres:
        (x_ref, gn_ref, mu_ref, wr_ref, wk_ref, wv_ref, w0_ref, w1_ref, w2_ref, a0_ref, a1_ref,
         a2_ref, g1_ref, g2_ref, v0_ref, v1_ref, v2_ref, vf_ref,
         r_out, lw_out, k_out, v_out, a_out, g_out, carry) = refs
    else:
        (x_ref, gn_ref, mu_ref, wr_ref, wk_ref, wv_ref, w0_ref, w1_ref, w2_ref, a0_ref, a1_ref,
         a2_ref, g1_ref, g2_ref,
         r_out, lw_out, k_out, v_out, a_out, g_out, carry) = refs
    s = pl.program_id(1)
    tm = x_ref.shape[1]

    @pl.when(s == 0)
    def _():
        carry[...] = jnp.zeros_like(carry)

    h = _rms(x_ref[0], gn_ref[...])
    prev_last = carry[...]
    carry[...] = h[tm - 1:tm, :]
    row = lax.broadcasted_iota(jnp.int32, (tm, 1), 0)
    xx = jnp.where(row == 0, prev_last, pltpu.roll(h, 1, 0)) - h

    def mix(i):
        return (h + xx * mu_ref[i:i + 1, :]).astype(BF16)

    xr, xw, xk, xv, xa, xg = [mix(i) for i in range(6)]
    r_out[0] = _dot(xr, wr_ref[...]).astype(BF16)
    k_out[0] = _dot(xk, wk_ref[...]).astype(BF16)
    v = _dot(xv, wv_ref[...])
    wl = w0_ref[...] + _dot(jnp.tanh(_dot(xw, w1_ref[...])).astype(BF16), w2_ref[...])
    w = -(jnp.maximum(-wl, 0.0) + jnp.log(1.0 + jnp.exp(-jnp.abs(wl)))) - 0.5
    lw_out[0] = -jnp.exp(w)
    a_out[0] = _sigmoid(a0_ref[...] + _dot(_dot(xa, a1_ref[...]).astype(BF16), a2_ref[...])).astype(BF16)
    g_out[0] = _dot(_sigmoid(_dot(xg, g1_ref[...])).astype(BF16), g2_ref[...]).astype(BF16)
    if has_vres:
        gate = _sigmoid(v0_ref[...] + _dot(_dot(xv, v1_ref[...]).astype(BF16), v2_ref[...]))
        v = v + (vf_ref[0].astype(F32) - v) * gate
    v_out[0] = v.astype(BF16)


def _rwkv_proj(x, gn, mu, wr, wk, wv, w0, w1, w2, a0, a1, a2, g1, g2, vres, v_first, tm=256):
    B, S, D = x.shape
    has_vres = vres is not None
    row = lambda a: a.reshape(1, -1)
    full = lambda a: pl.BlockSpec(a.shape, lambda b, s: (0,) * a.ndim)
    tok = pl.BlockSpec((1, tm, D), lambda b, s: (b, s, 0))
    ins = [x, row(gn), mu, wr.astype(BF16), wk.astype(BF16), wv.astype(BF16), row(w0),
           w1.astype(BF16), w2.astype(BF16), row(a0), a1.astype(BF16), a2.astype(BF16),
           g1.astype(BF16), g2.astype(BF16)]
    if has_vres:
        v0, v1, v2 = vres
        ins += [row(v0), v1.astype(BF16), v2.astype(BF16)]
    specs = [tok] + [full(a) for a in ins[1:]]
    if has_vres:
        ins.append(v_first)
        specs.append(tok)
    out_shape = [jax.ShapeDtypeStruct((B, S, D), dt) for dt in (BF16, F32, BF16, BF16, BF16, BF16)]
    return pl.pallas_call(
        functools.partial(_rwkv_proj_kernel, has_vres),
        grid=(B, S // tm),
        in_specs=specs,
        out_specs=[tok] * 6,
        out_shape=out_shape,
        scratch_shapes=[pltpu.VMEM((1, D), F32)],
        compiler_params=_params("arbitrary", "arbitrary"),
        name="rwkv_proj_vres" if has_vres else "rwkv_proj",
    )(*ins)


def _head_sum(x, ones_b):
    hi = x.astype(BF16)
    lo = (x - hi.astype(F32)).astype(BF16)
    return _dot(hi, ones_b) + _dot(lo, ones_b)


def _wkv_kernel(r_ref, lw_ref, k_ref, v_ref, a_ref, g_ref, kk_ref, ka_ref, rk_ref, lnw_ref, lnb_ref,
                z_ref, state):
    C = WKV_CHUNK
    P = 2 * C
    tt = r_ref.shape[1]
    nc = tt // C

    @pl.when(pl.program_id(2) == 0)
    def _():
        state[...] = jnp.zeros_like(state)

    lane = lax.broadcasted_iota(jnp.int32, (C, LANES), 1)
    m0 = (lane < HEAD_DIM).astype(F32)
    m1 = 1.0 - m0
    ri = lax.broadcasted_iota(jnp.int32, (P, P), 0)
    ci = lax.broadcasted_iota(jnp.int32, (P, P), 1)
    strict = ri > ci
    incl = ri >= ci
    eye = (ri == ci).astype(F32)
    tri = (lax.broadcasted_iota(jnp.int32, (C, C), 0) >= lax.broadcasted_iota(jnp.int32, (C, C), 1)).astype(F32)
    ones_b = _pair_ones().astype(BF16)

    def stack(x):
        return jnp.concatenate([x * m0, x * m1], axis=0).astype(BF16)

    nq = r_ref.shape[2] // LANES
    cs = range(nc)
    pre = []
    for q in range(nq):
        ls = slice(q * LANES, (q + 1) * LANES)
        r = r_ref[0, :, ls].astype(F32)
        k = k_ref[0, :, ls].astype(F32)
        v = v_ref[0, :, ls].astype(F32)
        a = a_ref[0, :, ls].astype(F32)
        lw = lw_ref[0, :, ls]
        kk = k * kk_ref[:, ls]
        kk = kk * lax.rsqrt(jnp.maximum(_head_sum(kk * kk, ones_b), 1e-24))
        k2 = k * (1.0 + (a - 1.0) * ka_ref[:, ls])
        bv = kk * a
        cum_c = _dot_hi(tri, jnp.concatenate([lw[c * C:(c + 1) * C] for c in cs], axis=1))
        cum = jnp.concatenate([cum_c[:, c * LANES:(c + 1) * LANES] for c in cs], axis=0)
        cum_end = [cum_c[C - 1:C, c * LANES:(c + 1) * LANES] for c in cs]
        to_end = jnp.concatenate([jnp.broadcast_to(ce, (C, LANES)) for ce in cum_end], axis=0) - cum
        g_inv = jnp.exp(-cum)
        g_end = jnp.exp(to_end)
        pre.append(dict(r=r, k2=k2, v=v, cum_end=cum_end, a_t=-kk * jnp.exp(cum - lw), r_t=r * jnp.exp(cum),
                        b_t=bv * g_inv, k_t=k2 * g_inv, b_h=bv * g_end, k_h=k2 * g_end))

    units = [(q, c) for c in cs for q in range(nq)]
    us = range(len(units))

    def stacks(name):
        return [stack(pre[q][name][c * C:(c + 1) * C]) for q, c in units]

    a_s, r_s, b_s, k_s, v_s, bh_s, kh_s = (stacks(x) for x in ("a_t", "r_t", "b_t", "k_t", "v", "b_h", "k_h"))
    n_ab = [jnp.where(strict, _dot_nt(a_s[u], b_s[u]), 0.0) for u in us]
    a_ak = [jnp.where(strict, _dot_nt(a_s[u], k_s[u]), 0.0).astype(BF16) for u in us]
    a_rb = [jnp.where(incl, _dot_nt(r_s[u], b_s[u]), 0.0).astype(BF16) for u in us]
    a_rk = [jnp.where(incl, _dot_nt(r_s[u], k_s[u]), 0.0).astype(BF16) for u in us]
    inv = [eye + n for n in n_ab]
    pw = [n.astype(BF16) for n in n_ab]
    for _ in range(5):
        pw = [_dot(p, p).astype(BF16) for p in pw]
        inv = [t + _dot(t.astype(BF16), p) for t, p in zip(inv, pw)]
    inv = [t.astype(BF16) for t in inv]
    w = [_dot(inv[u], a_s[u]).astype(BF16) for u in us]
    akv = [_dot(a_ak[u], v_s[u]).astype(BF16) for u in us]
    u0 = [_dot(inv[u], akv[u]).astype(BF16) for u in us]
    m_st = [_dot_tn(w[u], bh_s[u]).astype(BF16) for u in us]
    z_st = [_dot_tn(u0[u], bh_s[u]) + _dot_tn(v_s[u], kh_s[u]) for u in us]
    q_h = [(r_s[u].astype(F32) + _dot(a_rb[u], w[u])).astype(BF16) for u in us]
    y0 = [_dot(a_rb[u], u0[u]) + _dot(a_rk[u], v_s[u]) for u in us]

    s = [state[q] for q in range(nq)]
    ys = [[] for _ in range(nq)]
    for c in cs:
        for q in range(nq):
            u = c * nq + q
            sb = s[q].astype(BF16)
            y_s = _dot_nt(q_h[u], sb) + y0[u]
            ys[q].append(y_s[:C] + y_s[C:])
            s[q] = s[q] * jnp.exp(pre[q]["cum_end"][c]) + _dot(sb, m_st[u]) + z_st[u]
    for q in range(nq):
        state[q] = s[q]

    for q in range(nq):
        ls = slice(q * LANES, (q + 1) * LANES)
        y = jnp.concatenate(ys[q], axis=0)
        mean = _head_sum(y, ones_b) * (1.0 / HEAD_DIM)
        yc = y - mean
        var = _head_sum(yc * yc, ones_b) * (1.0 / HEAD_DIM)
        yn = yc * lax.rsqrt(var + GN_EPS) * lnw_ref[:, ls] + lnb_ref[:, ls]
        bonus = _head_sum(pre[q]["r"] * pre[q]["k2"] * rk_ref[:, ls], ones_b) * pre[q]["v"]
        z_ref[0, :, ls] = ((yn + bonus) * g_ref[0, :, ls].astype(F32)).astype(BF16)


def _wkv(r, lw, k, v, a, g, k_k, k_a, r_k, ln_w, ln_b):
    B, S, D = r.shape
    tt = min(WKV_TILE, S)
    width = WKV_PAIRS * LANES
    tok = pl.BlockSpec((1, tt, width), lambda b, p, t: (b, t, p))
    par = pl.BlockSpec((1, width), lambda b, p, t: (0, p))
    row = lambda x: x.reshape(1, D)
    return pl.pallas_call(
        _wkv_kernel,
        grid=(B, D // width, S // tt),
        in_specs=[tok] * 6 + [par] * 5,
        out_specs=tok,
        out_shape=jax.ShapeDtypeStruct((B, S, D), BF16),
        scratch_shapes=[pltpu.VMEM((WKV_PAIRS, LANES, LANES), F32)],
        compiler_params=_params("arbitrary", "arbitrary", "arbitrary"),
        name="wkv7",
    )(r, lw, k, v, a, g, row(k_k), row(k_a), row(r_k), row(ln_w), row(ln_b))


def _proj_res_kernel(gated, *refs):
    if gated:
        a_ref, gate_ref, w_ref, res_ref, o_ref = refs
        a = (a_ref[...].astype(F32) * _sigmoid(gate_ref[...].astype(F32))).astype(BF16)
    else:
        a_ref, w_ref, res_ref, o_ref = refs
        a = a_ref[...]
    o_ref[...] = res_ref[...] + _dot(a, w_ref[...])


def _proj_res(a, w, res, gate=None, tm=512):
    n, kdim = a.shape
    d = w.shape[1]
    gated = gate is not None
    ins = [a] + ([gate] if gated else []) + [w.astype(BF16), res]
    specs = ([pl.BlockSpec((tm, kdim), lambda i: (i, 0))] * (2 if gated else 1)
             + [pl.BlockSpec((kdim, d), lambda i: (0, 0)), pl.BlockSpec((tm, d), lambda i: (i, 0))])
    return pl.pallas_call(
        functools.partial(_proj_res_kernel, gated),
        grid=(n // tm,),
        in_specs=specs,
        out_specs=pl.BlockSpec((tm, d), lambda i: (i, 0)),
        out_shape=jax.ShapeDtypeStruct((n, d), F32),
        compiler_params=_params("arbitrary"),
        name="proj_res_gated" if gated else "proj_res",
    )(*ins)


def _ffn_kernel(x_ref, gn_ref, w1_ref, w3_ref, w2_ref, o_ref, h_ref, acc_ref):
    f = pl.program_id(1)

    @pl.when(f == 0)
    def _():
        h_ref[...] = _rms(x_ref[...], gn_ref[...]).astype(BF16)

    h = h_ref[...]
    h1 = _dot(h, w1_ref[...])
    act = (h1 * _sigmoid(h1) * _dot(h, w3_ref[...])).astype(BF16)
    part = _dot(act, w2_ref[...])

    @pl.when(f == 0)
    def _():
        acc_ref[...] = part

    @pl.when(f > 0)
    def _():
        acc_ref[...] += part

    @pl.when(f == pl.num_programs(1) - 1)
    def _():
        o_ref[...] = x_ref[...] + acc_ref[...]


def _ffn(x, gn, w1, w3, w2, tm=512, tf=1408):
    n, d = x.shape
    fdim = w1.shape[1]
    return pl.pallas_call(
        _ffn_kernel,
        grid=(n // tm, fdim // tf),
        in_specs=[pl.BlockSpec((tm, d), lambda i, f: (i, 0)),
                  pl.BlockSpec((1, d), lambda i, f: (0, 0)),
                  pl.BlockSpec((d, tf), lambda i, f: (0, f)),
                  pl.BlockSpec((d, tf), lambda i, f: (0, f)),
                  pl.BlockSpec((tf, d), lambda i, f: (f, 0))],
        out_specs=pl.BlockSpec((tm, d), lambda i, f: (i, 0)),
        out_shape=jax.ShapeDtypeStruct((n, d), F32),
        scratch_shapes=[pltpu.VMEM((tm, d), BF16), pltpu.VMEM((tm, d), F32)],
        compiler_params=_params("arbitrary", "arbitrary"),
        name="ffn_dense",
    )(x, gn.reshape(1, d), w1.astype(BF16), w3.astype(BF16), w2.astype(BF16))


def _router_kernel(x_ref, gn_ref, wr_ref, h_ref, route_ref):
    h = _rms(x_ref[...], gn_ref[...])
    h_ref[...] = h
    logits = _dot_hi(h, wr_ref[...])
    lane = lax.broadcasted_iota(jnp.int32, logits.shape, 1)
    neg = jnp.float32(-jnp.inf)
    logits = jnp.where(lane < N_EXPERTS, logits, neg)
    m1 = jnp.max(logits, axis=-1, keepdims=True)
    i1 = jnp.min(jnp.where(logits == m1, lane, LANES), axis=-1, keepdims=True)
    rest = jnp.where(lane == i1, neg, logits)
    m2 = jnp.max(rest, axis=-1, keepdims=True)
    i2 = jnp.min(jnp.where(rest == m2, lane, LANES), axis=-1, keepdims=True)
    e2 = jnp.exp(m2 - m1)
    p1 = 1.0 / (1.0 + e2)
    p2 = e2 * p1
    route_ref[...] = jnp.where(lane == 0, i1.astype(F32),
                     jnp.where(lane == 1, i2.astype(F32),
                     jnp.where(lane == 2, p1, jnp.where(lane == 3, p2, 0.0))))


def _router(x, gn, w_router, tm=512):
    n, d = x.shape
    wr = jnp.zeros((d, LANES), F32).at[:, :N_EXPERTS].set(w_router)
    return pl.pallas_call(
        _router_kernel,
        grid=(n // tm,),
        in_specs=[pl.BlockSpec((tm, d), lambda i: (i, 0)),
                  pl.BlockSpec((1, d), lambda i: (0, 0)),
                  pl.BlockSpec((d, LANES), lambda i: (0, 0))],
        out_specs=[pl.BlockSpec((tm, d), lambda i: (i, 0)), pl.BlockSpec((tm, LANES), lambda i: (i, 0))],
        out_shape=[jax.ShapeDtypeStruct((n, d), F32), jax.ShapeDtypeStruct((n, LANES), F32)],
        compiler_params=_params("arbitrary"),
        name="moe_router",
    )(x, gn.reshape(1, d), wr)


def _gather_kernel(idx_ref, src_ref, o_ref, sem):
    rows = o_ref.shape[0]
    base = pl.program_id(0) * rows

    def row_copy(j):
        return pltpu.make_async_copy(src_ref.at[pl.ds(idx_ref[base + j], 1)], o_ref.at[pl.ds(j, 1)], sem)

    def start(j, c):
        row_copy(j).start()
        return c

    def wait(j, c):
        row_copy(j).wait()
        return c

    lax.fori_loop(0, rows, start, 0, unroll=8)
    lax.fori_loop(0, rows, wait, 0, unroll=8)


def _gather_rows(src, idx, rows=GATHER_ROWS):
    m = idx.shape[0]
    d = src.shape[1]
    return pl.pallas_call(
        _gather_kernel,
        grid_spec=pltpu.PrefetchScalarGridSpec(
            num_scalar_prefetch=1,
            grid=(m // rows,),
            in_specs=[pl.BlockSpec(memory_space=pl.ANY)],
            out_specs=pl.BlockSpec((rows, d), lambda i, idx: (i, 0)),
            scratch_shapes=[pltpu.SemaphoreType.DMA(())],
        ),
        out_shape=jax.ShapeDtypeStruct((m, d), src.dtype),
        compiler_params=_params("arbitrary"),
        name="gather_rows",
    )(idx, src)


def _experts_kernel(be_ref, nu_ref, x_ref, w1_ref, w3_ref, w2_ref, o_ref, xb_ref, acc_ref):
    i, f = pl.program_id(0), pl.program_id(1)
    used = i < nu_ref[0]

    @pl.when(jnp.logical_and(used, f == 0))
    def _():
        xb_ref[...] = x_ref[...].astype(BF16)

    @pl.when(used)
    def _():
        x = xb_ref[...]
        h1 = _dot(x, w1_ref[0])
        act = (h1 * _sigmoid(h1) * _dot(x, w3_ref[0])).astype(BF16)
        part = _dot(act, w2_ref[0])

        @pl.when(f == 0)
        def _():
            acc_ref[...] = part

        @pl.when(f > 0)
        def _():
            acc_ref[...] += part

    @pl.when(f == pl.num_programs(1) - 1)
    def _():
        o_ref[...] = jnp.where(used, acc_ref[...], 0.0)


def _experts(xs, blk_e, n_used, w1, w3, w2, tf=512):
    cap, d = xs.shape
    edim = w1.shape[2]
    nb = cap // MOE_BM
    return pl.pallas_call(
        _experts_kernel,
        grid_spec=pltpu.PrefetchScalarGridSpec(
            num_scalar_prefetch=2,
            grid=(nb, edim // tf),
            in_specs=[pl.BlockSpec((MOE_BM, d), lambda i, f, be, nu: (i, 0)),
                      pl.BlockSpec((1, d, tf), lambda i, f, be, nu: (be[i], 0, f)),
                      pl.BlockSpec((1, d, tf), lambda i, f, be, nu: (be[i], 0, f)),
                      pl.BlockSpec((1, tf, d), lambda i, f, be, nu: (be[i], f, 0))],
            out_specs=pl.BlockSpec((MOE_BM, d), lambda i, f, be, nu: (i, 0)),
            scratch_shapes=[pltpu.VMEM((MOE_BM, d), BF16), pltpu.VMEM((MOE_BM, d), F32)],
        ),
        out_shape=jax.ShapeDtypeStruct((cap, d), F32),
        compiler_params=_params("arbitrary", "arbitrary"),
        name="moe_experts",
    )(blk_e, n_used, xs, w1.astype(BF16), w3.astype(BF16), w2.astype(BF16))


def _combine_kernel(x_ref, route_ref, y0_ref, y1_ref, o_ref):
    p1 = route_ref[:, 2:3]
    p2 = route_ref[:, 3:4]
    o_ref[...] = x_ref[...] + (p1 * y0_ref[...] + p2 * y1_ref[...])


def _combine(x, route, yg, tm=512):
    n, d = x.shape
    nt = n // tm
    return pl.pallas_call(
        _combine_kernel,
        grid=(nt,),
        in_specs=[pl.BlockSpec((tm, d), lambda i: (i, 0)),
                  pl.BlockSpec((tm, LANES), lambda i: (i, 0)),
                  pl.BlockSpec((tm, d), lambda i: (i, 0)),
                  pl.BlockSpec((tm, d), lambda i: (i + nt, 0))],
        out_specs=pl.BlockSpec((tm, d), lambda i: (i, 0)),
        out_shape=jax.ShapeDtypeStruct((n, d), F32),
        compiler_params=_params("arbitrary"),
        name="moe_combine",
    )(x, route, yg, yg)


def _moe(x, gn, w_router, w1, w3, w2):
    n, d = x.shape
    h, route = _router(x, gn, w_router)
    e = route[:, :2].astype(jnp.int32).reshape(-1)
    onehot = (e[:, None] == jnp.arange(N_EXPERTS, dtype=jnp.int32)[None, :]).astype(jnp.int32)
    csum = jnp.cumsum(onehot, axis=0)
    rank = jnp.take_along_axis(csum, e[:, None], axis=1)[:, 0] - 1
    counts = csum[-1]
    padded = ((counts + MOE_BM - 1) // MOE_BM) * MOE_BM
    pend = jnp.cumsum(padded)
    dest = (pend - padded)[e] + rank
    nb = -(-(2 * n) // MOE_BM) + N_EXPERTS
    cap = nb * MOE_BM
    rows_tok = jnp.zeros((cap,), jnp.int32).at[dest].set(jnp.arange(2 * n, dtype=jnp.int32) // 2)
    blk_e = jnp.minimum(jnp.searchsorted(pend, jnp.arange(nb, dtype=jnp.int32) * MOE_BM, side="right"),
                        N_EXPERTS - 1).astype(jnp.int32)
    n_used = (pend[-1:] // MOE_BM).astype(jnp.int32)
    xs = _gather_rows(h, rows_tok)
    yb = _experts(xs, blk_e, n_used, w1, w3, w2)
    dest2 = dest.reshape(n, 2)
    yg = _gather_rows(yb, jnp.concatenate([dest2[:, 0], dest2[:, 1]]))
    return _combine(x, route, yg)


def _kv_kernel(x_ref, gn_ref, wk_ref, wv_ref, wf_ref, bf_ref, kn_ref, k_out, v_out, dp_out, carry):
    s = pl.program_id(1)
    tm = x_ref.shape[1]

    @pl.when(s == 0)
    def _():
        carry[...] = jnp.zeros_like(carry)

    h = _rms(x_ref[0], gn_ref[...])
    hb = h.astype(BF16)
    k_out[0] = _head_rms(_dot(hb, wk_ref[...]), kn_ref[...]).astype(BF16)
    v_out[0] = _dot(hb, wv_ref[...]).astype(BF16)
    fl = _dot(hb, wf_ref[...]) + bf_ref[...]
    log_f = jnp.minimum(fl, 0.0) - jnp.log(1.0 + jnp.exp(-jnp.abs(fl)))
    tri = (lax.broadcasted_iota(jnp.int32, (tm, tm), 0) >= lax.broadcasted_iota(jnp.int32, (tm, tm), 1)).astype(F32)
    dcum = carry[...] + _dot_hi(tri, log_f)
    carry[...] = dcum[tm - 1:tm, :]
    rest = -dcum * LOG2_E
    pieces = []
    for _ in range(N_BIAS):
        piece = rest.astype(BF16)
        pieces.append(piece)
        rest = rest - piece.astype(F32)
    dp_out[0] = jnp.concatenate(pieces, axis=1)


def _shared_kv(x, gn, w_kvf, b_f, k_norm, tm=256):
    B, S, D = x.shape
    wk = w_kvf[:, :D].astype(BF16)
    wv = w_kvf[:, D:2 * D].astype(BF16)
    wf = jnp.zeros((D, LANES), F32).at[:, :N_HEADS].set(w_kvf[:, 2 * D:]).astype(BF16)
    bf = jnp.zeros((1, LANES), F32).at[0, :N_HEADS].set(b_f)
    kn = jnp.tile(k_norm, N_HEADS).reshape(1, D)
    tok = pl.BlockSpec((1, tm, D), lambda b, s: (b, s, 0))
    full = lambda a: pl.BlockSpec(a.shape, lambda b, s: (0,) * a.ndim)
    ins = [x, gn.reshape(1, D), wk, wv, wf, bf, kn]
    return pl.pallas_call(
        _kv_kernel,
        grid=(B, S // tm),
        in_specs=[tok] + [full(a) for a in ins[1:]],
        out_specs=[tok, tok, pl.BlockSpec((1, tm, N_BIAS * LANES), lambda b, s: (b, s, 0))],
        out_shape=[jax.ShapeDtypeStruct((B, S, D), BF16), jax.ShapeDtypeStruct((B, S, D), BF16),
                   jax.ShapeDtypeStruct((B, S, N_BIAS * LANES), BF16)],
        scratch_shapes=[pltpu.VMEM((1, LANES), F32)],
        compiler_params=_params("arbitrary", "arbitrary"),
        name="shared_kv",
    )(*ins)


def _qg_kernel(x_ref, gn_ref, wq_ref, wg_ref, qn_ref, q_out, og_out):
    hb = _rms(x_ref[...], gn_ref[...]).astype(BF16)
    q = _head_rms(_dot(hb, wq_ref[...]), qn_ref[...])
    q_out[...] = (q * (HEAD_DIM ** -0.5 * LOG2_E)).astype(BF16)
    og_out[...] = _dot(hb, wg_ref[...]).astype(BF16)


def _qg_proj(x, gn, w_qg, q_norm, tm=512):
    n, d = x.shape
    qn = jnp.tile(q_norm, N_HEADS).reshape(1, d)
    tok = pl.BlockSpec((tm, d), lambda i: (i, 0))
    const = lambda a: pl.BlockSpec(a.shape, lambda i: (0, 0))
    ins = [x, gn.reshape(1, d), w_qg[:, :d].astype(BF16), w_qg[:, d:].astype(BF16), qn]
    return pl.pallas_call(
        _qg_kernel,
        grid=(n // tm,),
        in_specs=[tok] + [const(a) for a in ins[1:]],
        out_specs=[tok, tok],
        out_shape=[jax.ShapeDtypeStruct((n, d), BF16)] * 2,
        compiler_params=_params("arbitrary"),
        name="fox_qg",
    )(*ins)


N_BIAS = 3


def _attn_kernel(qt_ref, k_ref, vt_ref, o_ref):
    tq, tk = ATT_TQ, ATT_TK
    i = pl.program_id(2)
    hs = range(ATT_HEADS)
    qt = [qt_ref[0, h] for h in hs]
    q_pos = i * tq + lax.broadcasted_iota(jnp.int32, (tk, tq), 1)
    k_off = lax.broadcasted_iota(jnp.int32, (tk, tq), 0)

    def step(j, carry, masked):
        off = pl.multiple_of(j * tk, tk)
        s = [_dot(k_ref[0, h, pl.ds(off, tk), :], qt[h]) for h in hs]
        if masked:
            keep = k_off + off <= q_pos
            s = [jnp.where(keep, x, -jnp.inf) for x in s]
        m_new = [jnp.maximum(carry[h][0], jnp.max(s[h], axis=0, keepdims=True)) for h in hs]
        alpha = [jnp.exp2(carry[h][0] - m_new[h]) for h in hs]
        p = [jnp.exp2(s[h] - m_new[h]) for h in hs]
        l = [alpha[h] * carry[h][1] + jnp.sum(p[h], axis=0, keepdims=True) for h in hs]
        pv = [_dot(vt_ref[0, h, :, pl.ds(off, tk)], p[h].astype(BF16)) for h in hs]
        return tuple((m_new[h], l[h], alpha[h] * carry[h][2] + pv[h]) for h in hs)

    init = tuple((jnp.full((1, tq), -jnp.inf, F32), jnp.zeros((1, tq), F32), jnp.zeros((HEAD_DIM, tq), F32))
                 for _ in hs)
    carry = lax.fori_loop(0, i, lambda j, c: step(j, c, False), init)
    carry = step(i, carry, True)
    for h in hs:
        _, l, acc = carry[h]
        o_ref[0, h] = (acc / l).astype(BF16)


def _fox_attn(qt, k_aug, vt):
    B, H, _, S = qt.shape
    g = ATT_HEADS
    return pl.pallas_call(
        _attn_kernel,
        grid=(B, H // g, S // ATT_TQ),
        in_specs=[pl.BlockSpec((1, g, LANES, ATT_TQ), lambda b, h, i: (b, h, 0, i)),
                  pl.BlockSpec((1, g, S, LANES), lambda b, h, i: (b, h, 0, 0)),
                  pl.BlockSpec((1, g, HEAD_DIM, S), lambda b, h, i: (b, h, 0, 0))],
        out_specs=pl.BlockSpec((1, g, HEAD_DIM, ATT_TQ), lambda b, h, i: (b, h, 0, i)),
        out_shape=jax.ShapeDtypeStruct((B, H, HEAD_DIM, S), BF16),
        compiler_params=_params("arbitrary", "arbitrary", "arbitrary"),
        name="fox_attn",
    )(qt, k_aug, vt)


def _split_heads_t(a, B, S):
    return a.reshape(B, S, N_HEADS, HEAD_DIM).transpose(0, 2, 3, 1)


def _attn_kv_layout(k_sh, v_sh, d_pieces):
    B, S, _ = k_sh.shape
    kh = k_sh.reshape(B, S, N_HEADS, HEAD_DIM).transpose(0, 2, 1, 3)
    dp = d_pieces.reshape(B, S, N_BIAS, LANES)[..., :N_HEADS].transpose(0, 3, 1, 2)
    pad = jnp.zeros((B, N_HEADS, S, LANES - HEAD_DIM - N_BIAS), BF16)
    return jnp.concatenate([kh, dp, pad], axis=-1), _split_heads_t(v_sh, B, S)


def _attn_q_layout(q, B, S):
    qt = _split_heads_t(q, B, S)
    ones = jnp.ones((B, N_HEADS, N_BIAS, S), BF16)
    pad = jnp.zeros((B, N_HEADS, LANES - HEAD_DIM - N_BIAS, S), BF16)
    return jnp.concatenate([qt, ones, pad], axis=2)


def kernel(x, norm_mix, norm_ffn, rk_mu, rk_w_r, rk_w_k, rk_w_v, rk_w_o, rk_w0, rk_w1, rk_w2, rk_a0, rk_a1, rk_a2, rk_v0, rk_v1, rk_v2, rk_g1, rk_g2, rk_k_k, rk_k_a, rk_r_k, rk_lnx_w, rk_lnx_b, kv_norm, w_kvf, b_f, k_norm, fx_w_qg, fx_q_norm, fx_w_o, ffn_w1, ffn_w3, ffn_w2, moe_router, moe_w1, moe_w3, moe_w2):
    B, S, D = x.shape
    n = B * S
    depth = norm_mix.shape[0]
    n_rwkv = rk_mu.shape[0]
    v_first = None
    kv = None
    for l in range(depth):
        if l < n_rwkv:
            vres = None if l == 0 else (rk_v0[l - 1], rk_v1[l - 1], rk_v2[l - 1])
            r, lw, k, v, a, g = _rwkv_proj(
                x, norm_mix[l], rk_mu[l], rk_w_r[l], rk_w_k[l], rk_w_v[l], rk_w0[l], rk_w1[l], rk_w2[l],
                rk_a0[l], rk_a1[l], rk_a2[l], rk_g1[l], rk_g2[l], vres, v_first)
            if l == 0:
                v_first = v
            z = _wkv(r, lw, k, v, a, g, rk_k_k[l], rk_k_a[l], rk_r_k[l], rk_lnx_w[l], rk_lnx_b[l])
            x2 = _proj_res(z.reshape(n, D), rk_w_o[l], x.reshape(n, D))
        else:
            if kv is None:
                k_sh, v_sh, d_pieces = _shared_kv(x, kv_norm, w_kvf, b_f, k_norm)
                kv = _attn_kv_layout(k_sh, v_sh, d_pieces)
            j = l - n_rwkv
            q, og = _qg_proj(x.reshape(n, D), norm_mix[l], fx_w_qg[j], fx_q_norm[j])
            ot = _fox_attn(_attn_q_layout(q, B, S), *kv)
            o = ot.transpose(0, 3, 1, 2).reshape(n, D)
            x2 = _proj_res(o, fx_w_o[j], x.reshape(n, D), gate=og)
        i = l // 2
        if l % 2 == 0:
            x2 = _ffn(x2, norm_ffn[l], ffn_w1[i], ffn_w3[i], ffn_w2[i])
        else:
            x2 = _moe(x2, norm_ffn[l], moe_router[i], moe_w1[i], moe_w3[i], moe_w2[i])
        x = x2.reshape(B, S, D)
    return x
```

```python
import functools

import jax
import jax.numpy as jnp
from jax import lax
from jax.experimental import pallas as pl
from jax.experimental.pallas import tpu as pltpu

F32 = jnp.float32
BF16 = jnp.bfloat16
HI = lax.Precision.HIGHEST

D_MODEL = 1024
HEAD_DIM = 64
N_HEADS = D_MODEL // HEAD_DIM
LANES = 128
HEAD_PAIRS = D_MODEL // LANES
N_EXPERTS = 8
LOG2_E = 1.4426950408889634
RMS_EPS = 1e-6
GN_EPS = 64e-5
VMEM_LIMIT_BYTES = 56 * 1024 * 1024

WKV_CHUNK = 64
WKV_TILE = 512
WKV_PAIRS = 2
ATT_TQ = 512
ATT_TK = 512
ATT_HEADS = 4
MOE_BM = 1024
MOE_TF = 512


def _params(*sem):
    return pltpu.CompilerParams(dimension_semantics=sem, vmem_limit_bytes=VMEM_LIMIT_BYTES)


def _dot(a, b):
    return jnp.dot(a, b, preferred_element_type=F32)


def _dot_hi(a, b):
    return jnp.dot(a, b, preferred_element_type=F32, precision=HI)


def _dot_nt(a, b):
    return lax.dot_general(a, b, (((1,), (1,)), ((), ())), preferred_element_type=F32)


def _dot_tn(a, b):
    return lax.dot_general(a, b, (((0,), (0,)), ((), ())), preferred_element_type=F32)


def _rms(x, g):
    return x * lax.rsqrt(jnp.mean(x * x, axis=-1, keepdims=True) + RMS_EPS) * g


def _sigmoid(x):
    return 1.0 / (1.0 + jnp.exp(-x))


def _pair_ones():
    ri = lax.broadcasted_iota(jnp.int32, (LANES, LANES), 0)
    ci = lax.broadcasted_iota(jnp.int32, (LANES, LANES), 1)
    return ((ri < HEAD_DIM) == (ci < HEAD_DIM)).astype(F32)


def _head_rms(x, gain_row):
    ones = _pair_ones()
    parts = []
    for p in range(HEAD_PAIRS):
        xs = x[:, p * LANES:(p + 1) * LANES]
        ms = _dot_hi(xs * xs, ones) * (1.0 / HEAD_DIM)
        parts.append(xs * lax.rsqrt(ms + RMS_EPS))
    return jnp.concatenate(parts, axis=1) * gain_row


def _rwkv_proj_kernel(has_vres, *refs):
    if has_vres:
        (x_ref, gn_ref, mu_ref, wr_ref, wk_ref, wv_ref, w0_ref, w1_ref, w2_ref, a0_ref, a1_ref,
         a2_ref, g1_ref, g2_ref, v0_ref, v1_ref, v2_ref, vf_ref,
         r_out, lw_out, k_out, v_out, a_out, g_out, carry) = refs
    else:
        (x_ref, gn_ref, mu_ref, wr_ref, wk_ref, wv_ref, w0_ref, w1_ref, w2_ref, a0_ref, a1_ref,
         a2_ref, g1_ref, g2_ref,
         r_out, lw_out, k_out, v_out, a_out, g_out, carry) = refs
    s = pl.program_id(1)
    tm = x_ref.shape[1]

    @pl.when(s == 0)
    def _():
        carry[...] = jnp.zeros_like(carry)

    h = _rms(x_ref[0], gn_ref[...])
    prev_last = carry[...]
    carry[...] = h[tm - 1:tm, :]
    row = lax.broadcasted_iota(jnp.int32, (tm, 1), 0)
    xx = jnp.where(row == 0, prev_last, pltpu.roll(h, 1, 0)) - h

    def mix(i):
        return (h + xx * mu_ref[i:i + 1, :]).astype(BF16)

    xr, xw, xk, xv, xa, xg = [mix(i) for i in range(6)]
    r_out[0] = _dot(xr, wr_ref[...]).astype(BF16)
    k_out[0] = _dot(xk, wk_ref[...]).astype(BF16)
    v = _dot(xv, wv_ref[...])
    wl = w0_ref[...] + _dot(jnp.tanh(_dot(xw, w1_ref[...])).astype(BF16), w2_ref[...])
    w = -(jnp.maximum(-wl, 0.0) + jnp.log(1.0 + jnp.exp(-jnp.abs(wl)))) - 0.5
    lw_out[0] = -jnp.exp(w)
    a_out[0] = _sigmoid(a0_ref[...] + _dot(_dot(xa, a1_ref[...]).astype(BF16), a2_ref[...])).astype(BF16)
    g_out[0] = _dot(_sigmoid(_dot(xg, g1_ref[...])).astype(BF16), g2_ref[...]).astype(BF16)
    if has_vres:
        gate = _sigmoid(v0_ref[...] + _dot(_dot(xv, v1_ref[...]).astype(BF16), v2_ref[...]))
        v = v + (vf_ref[0].astype(F32) - v) * gate
    v_out[0] = v.astype(BF16)


def _rwkv_proj(x, gn, mu, wr, wk, wv, w0, w1, w2, a0, a1, a2, g1, g2, vres, v_first, tm=256):
    B, S, D = x.shape
    has_vres = vres is not None
    row = lambda a: a.reshape(1, -1)
    full = lambda a: pl.BlockSpec(a.shape, lambda b, s: (0,) * a.ndim)
    tok = pl.BlockSpec((1, tm, D), lambda b, s: (b, s, 0))
    ins = [x, row(gn), mu, wr.astype(BF16), wk.astype(BF16), wv.astype(BF16), row(w0),
           w1.astype(BF16), w2.astype(BF16), row(a0), a1.astype(BF16), a2.astype(BF16),
           g1.astype(BF16), g2.astype(BF16)]
    if has_vres:
        v0, v1, v2 = vres
        ins += [row(v0), v1.astype(BF16), v2.astype(BF16)]
    specs = [tok] + [full(a) for a in ins[1:]]
    if has_vres:
        ins.append(v_first)
        specs.append(tok)
    out_shape = [jax.ShapeDtypeStruct((B, S, D), dt) for dt in (BF16, F32, BF16, BF16, BF16, BF16)]
    return pl.pallas_call(
        functools.partial(_rwkv_proj_kernel, has_vres),
        grid=(B, S // tm),
        in_specs=specs,
        out_specs=[tok] * 6,
        out_shape=out_shape,
        scratch_shapes=[pltpu.VMEM((1, D), F32)],
        compiler_params=_params("arbitrary", "arbitrary"),
        name="rwkv_proj_vres" if has_vres else "rwkv_proj",
    )(*ins)


def _head_sum(x, ones_b):
    hi = x.astype(BF16)
    lo = (x - hi.astype(F32)).astype(BF16)
    return _dot(hi, ones_b) + _dot(lo, ones_b)


def _wkv_kernel(r_ref, lw_ref, k_ref, v_ref, a_ref, g_ref, kk_ref, ka_ref, rk_ref, lnw_ref, lnb_ref,
                z_ref, state):
    C = WKV_CHUNK
    P = 2 * C
    tt = r_ref.shape[1]
    nc = tt // C

    @pl.when(pl.program_id(2) == 0)
    def _():
        state[...] = jnp.zeros_like(state)

    lane = lax.broadcasted_iota(jnp.int32, (C, LANES), 1)
    m0 = (lane < HEAD_DIM).astype(F32)
    m1 = 1.0 - m0
    ri = lax.broadcasted_iota(jnp.int32, (P, P), 0)
    ci = lax.broadcasted_iota(jnp.int32, (P, P), 1)
    strict = ri > ci
    incl = ri >= ci
    eye = (ri == ci).astype(F32)
    tri = (lax.broadcasted_iota(jnp.int32, (C, C), 0) >= lax.broadcasted_iota(jnp.int32, (C, C), 1)).astype(F32)
    ones_b = _pair_ones().astype(BF16)

    def stack(x):
        return jnp.concatenate([x * m0, x * m1], axis=0).astype(BF16)

    nq = r_ref.shape[2] // LANES
    cs = range(nc)
    pre = []
    for q in range(nq):
        ls = slice(q * LANES, (q + 1) * LANES)
        r = r_ref[0, :, ls].astype(F32)
        k = k_ref[0, :, ls].astype(F32)
        v = v_ref[0, :, ls].astype(F32)
        a = a_ref[0, :, ls].astype(F32)
        lw = lw_ref[0, :, ls]
        kk = k * kk_ref[:, ls]
        kk = kk * lax.rsqrt(jnp.maximum(_head_sum(kk * kk, ones_b), 1e-24))
        k2 = k * (1.0 + (a - 1.0) * ka_ref[:, ls])
        bv = kk * a
        cum_c = _dot_hi(tri, jnp.concatenate([lw[c * C:(c + 1) * C] for c in cs], axis=1))
        cum = jnp.concatenate([cum_c[:, c * LANES:(c + 1) * LANES] for c in cs], axis=0)
        cum_end = [cum_c[C - 1:C, c * LANES:(c + 1) * LANES] for c in cs]
        to_end = jnp.concatenate([jnp.broadcast_to(ce, (C, LANES)) for ce in cum_end], axis=0) - cum
        g_inv = jnp.exp(-cum)
        g_end = jnp.exp(to_end)
        pre.append(dict(r=r, k2=k2, v=v, cum_end=cum_end, a_t=-kk * jnp.exp(cum - lw), r_t=r * jnp.exp(cum),
                        b_t=bv * g_inv, k_t=k2 * g_inv, b_h=bv * g_end, k_h=k2 * g_end))

    units = [(q, c) for c in cs for q in range(nq)]
    us = range(len(units))

    def stacks(name):
        return [stack(pre[q][name][c * C:(c + 1) * C]) for q, c in units]

    a_s, r_s, b_s, k_s, v_s, bh_s, kh_s = (stacks(x) for x in ("a_t", "r_t", "b_t", "k_t", "v", "b_h", "k_h"))
    cat0 = lambda x, y: jnp.concatenate([x, y], axis=0)
    cat1 = lambda x, y: jnp.concatenate([x, y], axis=1)
    gram = [_dot_nt(cat0(a_s[u], r_s[u]), cat0(b_s[u], k_s[u])) for u in us]
    n_ab = [jnp.where(strict, g[:P, :P], 0.0) for g in gram]
    a_ak = [jnp.where(strict, g[:P, P:], 0.0).astype(BF16) for g in gram]
    a_rb = [jnp.where(incl, g[P:, :P], 0.0).astype(BF16) for g in gram]
    a_rk = [jnp.where(incl, g[P:, P:], 0.0).astype(BF16) for g in gram]
    akv = [_dot(a_ak[u], v_s[u]).astype(BF16) for u in us]
    pw = [n.astype(BF16) for n in n_ab]
    inv = [eye + n for n in n_ab]
    pw = [_dot(p, p).astype(BF16) for p in pw]
    for _ in range(4):
        both = [_dot(cat0(pw[u], inv[u].astype(BF16)), pw[u]) for u in us]
        pw = [x[:P].astype(BF16) for x in both]
        inv = [inv[u] + both[u][P:] for u in us]
    inv = [(inv[u] + _dot(inv[u].astype(BF16), pw[u])).astype(BF16) for u in us]
    wu = [_dot(inv[u], cat1(a_s[u], akv[u])).astype(BF16) for u in us]
    rb = [_dot(a_rb[u], wu[u]) for u in us]
    q_h = [(r_s[u].astype(F32) + rb[u][:, :LANES]).astype(BF16) for u in us]
    y0 = [rb[u][:, LANES:] + _dot(a_rk[u], v_s[u]) for u in us]
    m_st = [_dot_tn(wu[u][:, :LANES], bh_s[u]).astype(BF16) for u in us]
    z_st = [_dot_tn(cat0(wu[u][:, LANES:], v_s[u]), cat0(bh_s[u], kh_s[u])) for u in us]

    s = [state[q] for q in range(nq)]
    ys = [[] for _ in range(nq)]
    for c in cs:
        for q in range(nq):
            u = c * nq + q
            sb = s[q].astype(BF16)
            y_s = _dot_nt(q_h[u], sb) + y0[u]
            ys[q].append(y_s[:C] + y_s[C:])
            s[q] = s[q] * jnp.exp(pre[q]["cum_end"][c]) + _dot(sb, m_st[u]) + z_st[u]
    for q in range(nq):
        state[q] = s[q]

    for q in range(nq):
        ls = slice(q * LANES, (q + 1) * LANES)
        y = jnp.concatenate(ys[q], axis=0)
        mean = _head_sum(y, ones_b) * (1.0 / HEAD_DIM)
        yc = y - mean
        var = _head_sum(yc * yc, ones_b) * (1.0 / HEAD_DIM)
        yn = yc * lax.rsqrt(var + GN_EPS) * lnw_ref[:, ls] + lnb_ref[:, ls]
        bonus = _head_sum(pre[q]["r"] * pre[q]["k2"] * rk_ref[:, ls], ones_b) * pre[q]["v"]
        z_ref[0, :, ls] = ((yn + bonus) * g_ref[0, :, ls].astype(F32)).astype(BF16)


def _wkv(r, lw, k, v, a, g, k_k, k_a, r_k, ln_w, ln_b):
    B, S, D = r.shape
    tt = min(WKV_TILE, S)
    width = WKV_PAIRS * LANES
    tok = pl.BlockSpec((1, tt, width), lambda b, p, t: (b, t, p))
    par = pl.BlockSpec((1, width), lambda b, p, t: (0, p))
    row = lambda x: x.reshape(1, D)
    return pl.pallas_call(
        _wkv_kernel,
        grid=(B, D // width, S // tt),
        in_specs=[tok] * 6 + [par] * 5,
        out_specs=tok,
        out_shape=jax.ShapeDtypeStruct((B, S, D), BF16),
        scratch_shapes=[pltpu.VMEM((WKV_PAIRS, LANES, LANES), F32)],
        compiler_params=_params("arbitrary", "arbitrary", "arbitrary"),
        name="wkv7",
    )(r, lw, k, v, a, g, row(k_k), row(k_a), row(r_k), row(ln_w), row(ln_b))


def _proj_res_kernel(gated, *refs):
    if gated:
        a_ref, gate_ref, w_ref, res_ref, o_ref = refs
        a = (a_ref[...].astype(F32) * _sigmoid(gate_ref[...].astype(F32))).astype(BF16)
    else:
        a_ref, w_ref, res_ref, o_ref = refs
        a = a_ref[...]
    o_ref[...] = res_ref[...] + _dot(a, w_ref[...])


def _proj_res(a, w, res, gate=None, tm=512):
    n, kdim = a.shape
    d = w.shape[1]
    gated = gate is not None
    ins = [a] + ([gate] if gated else []) + [w.astype(BF16), res]
    specs = ([pl.BlockSpec((tm, kdim), lambda i: (i, 0))] * (2 if gated else 1)
             + [pl.BlockSpec((kdim, d), lambda i: (0, 0)), pl.BlockSpec((tm, d), lambda i: (i, 0))])
    return pl.pallas_call(
        functools.partial(_proj_res_kernel, gated),
        grid=(n // tm,),
        in_specs=specs,
        out_specs=pl.BlockSpec((tm, d), lambda i: (i, 0)),
        out_shape=jax.ShapeDtypeStruct((n, d), F32),
        compiler_params=_params("arbitrary"),
        name="proj_res_gated" if gated else "proj_res",
    )(*ins)


def _ffn_kernel(x_ref, gn_ref, w1_ref, w3_ref, w2_ref, o_ref, h_ref, acc_ref):
    f = pl.program_id(1)

    @pl.when(f == 0)
    def _():
        h_ref[...] = _rms(x_ref[...], gn_ref[...]).astype(BF16)

    h = h_ref[...]
    h1 = _dot(h, w1_ref[...])
    act = (h1 * _sigmoid(h1) * _dot(h, w3_ref[...])).astype(BF16)
    part = _dot(act, w2_ref[...])

    @pl.when(f == 0)
    def _():
        acc_ref[...] = part

    @pl.when(f > 0)
    def _():
        acc_ref[...] += part

    @pl.when(f == pl.num_programs(1) - 1)
    def _():
        o_ref[...] = x_ref[...] + acc_ref[...]


def _ffn(x, gn, w1, w3, w2, tm=512, tf=1408):
    n, d = x.shape
    fdim = w1.shape[1]
    return pl.pallas_call(
        _ffn_kernel,
        grid=(n // tm, fdim // tf),
        in_specs=[pl.BlockSpec((tm, d), lambda i, f: (i, 0)),
                  pl.BlockSpec((1, d), lambda i, f: (0, 0)),
                  pl.BlockSpec((d, tf), lambda i, f: (0, f)),
                  pl.BlockSpec((d, tf), lambda i, f: (0, f)),
                  pl.BlockSpec((tf, d), lambda i, f: (f, 0))],
        out_specs=pl.BlockSpec((tm, d), lambda i, f: (i, 0)),
        out_shape=jax.ShapeDtypeStruct((n, d), F32),
        scratch_shapes=[pltpu.VMEM((tm, d), BF16), pltpu.VMEM((tm, d), F32)],
        compiler_params=_params("arbitrary", "arbitrary"),
        name="ffn_dense",
    )(x, gn.reshape(1, d), w1.astype(BF16), w3.astype(BF16), w2.astype(BF16))


def _router_kernel(x_ref, gn_ref, wr_ref, h_ref, route_ref, cnt_ref, carry):
    tm = x_ref.shape[0]

    @pl.when(pl.program_id(0) == 0)
    def _():
        carry[...] = jnp.zeros_like(carry)

    h = _rms(x_ref[...], gn_ref[...])
    h_ref[...] = h
    logits = _dot_hi(h, wr_ref[...])
    lane = lax.broadcasted_iota(jnp.int32, logits.shape, 1)
    neg = jnp.float32(-jnp.inf)
    logits = jnp.where(lane < N_EXPERTS, logits, neg)
    m1 = jnp.max(logits, axis=-1, keepdims=True)
    i1 = jnp.min(jnp.where(logits == m1, lane, LANES), axis=-1, keepdims=True)
    rest = jnp.where(lane == i1, neg, logits)
    m2 = jnp.max(rest, axis=-1, keepdims=True)
    i2 = jnp.min(jnp.where(rest == m2, lane, LANES), axis=-1, keepdims=True)
    e2 = jnp.exp(m2 - m1)
    p1 = 1.0 / (1.0 + e2)
    p2 = e2 * p1
    oh1 = (lane == i1).astype(F32)
    oh2 = (lane == i2).astype(F32)
    cnt = oh1 + oh2
    earlier = (lax.broadcasted_iota(jnp.int32, (tm, tm), 0) > lax.broadcasted_iota(jnp.int32, (tm, tm), 1))
    before = carry[...] + _dot(earlier.astype(BF16), cnt.astype(BF16))
    r1 = jnp.sum(oh1 * before, axis=-1, keepdims=True)
    r2 = jnp.sum(oh2 * before, axis=-1, keepdims=True)
    total = carry[...] + jnp.sum(cnt, axis=0, keepdims=True)
    carry[...] = total
    cnt_ref[...] = total
    cols = (i1.astype(F32), i2.astype(F32), p1, p2, r1, r2)
    route = jnp.zeros(logits.shape, F32)
    for c, val in enumerate(cols):
        route = jnp.where(lane == c, val, route)
    route_ref[...] = route


def _router(x, gn, w_router, tm=512):
    n, d = x.shape
    wr = jnp.zeros((d, LANES), F32).at[:, :N_EXPERTS].set(w_router)
    return pl.pallas_call(
        _router_kernel,
        grid=(n // tm,),
        in_specs=[pl.BlockSpec((tm, d), lambda i: (i, 0)),
                  pl.BlockSpec((1, d), lambda i: (0, 0)),
                  pl.BlockSpec((d, LANES), lambda i: (0, 0))],
        out_specs=[pl.BlockSpec((tm, d), lambda i: (i, 0)), pl.BlockSpec((tm, LANES), lambda i: (i, 0)),
                   pl.BlockSpec((1, LANES), lambda i: (0, 0))],
        out_shape=[jax.ShapeDtypeStruct((n, d), F32), jax.ShapeDtypeStruct((n, LANES), F32),
                   jax.ShapeDtypeStruct((1, LANES), F32)],
        scratch_shapes=[pltpu.VMEM((1, LANES), F32)],
        compiler_params=_params("arbitrary"),
        name="moe_router",
    )(x, gn.reshape(1, d), wr)


def _dispatch_kernel(dest_ref, pend_ref, h_ref, xs_ref, zeros, sem, zsem):
    i = pl.program_id(0)
    tm = h_ref.shape[0]
    n = pl.num_programs(0) * tm

    @pl.when(i == 0)
    def _():
        zeros[...] = jnp.zeros_like(zeros)

        def tail(e):
            start = pl.multiple_of(pend_ref[e] - MOE_BM, MOE_BM)
            return pltpu.make_async_copy(zeros, xs_ref.at[pl.ds(start, MOE_BM)], zsem)

        def nonempty(e):
            return pend_ref[e] > (pend_ref[e - 1] if e else 0)

        for e in range(N_EXPERTS):
            pl.when(nonempty(e))(lambda e=e: tail(e).start())
        for e in range(N_EXPERTS):
            pl.when(nonempty(e))(lambda e=e: tail(e).wait())

        def spare(b):
            return pltpu.make_async_copy(zeros, xs_ref.at[pl.ds(pl.multiple_of(b * MOE_BM, MOE_BM), MOE_BM)], zsem)

        first_spare = pend_ref[N_EXPERTS - 1] // MOE_BM
        n_blocks = xs_ref.shape[0] // MOE_BM
        lax.fori_loop(first_spare, n_blocks, lambda b, c: (spare(b).start(), c)[1], 0)
        lax.fori_loop(first_spare, n_blocks, lambda b, c: (spare(b).wait(), c)[1], 0)

    def row_copy(j, k):
        return pltpu.make_async_copy(h_ref.at[pl.ds(j, 1)],
                                     xs_ref.at[pl.ds(dest_ref[k * n + i * tm + j], 1)], sem)

    def start(j, c):
        row_copy(j, 0).start()
        row_copy(j, 1).start()
        return c

    def wait(j, c):
        row_copy(j, 0).wait()
        row_copy(j, 1).wait()
        return c

    lax.fori_loop(0, tm, start, 0, unroll=8)
    lax.fori_loop(0, tm, wait, 0, unroll=8)


def _dispatch(h, dest, pend, cap, tm=512):
    n, d = h.shape
    return pl.pallas_call(
        _dispatch_kernel,
        grid_spec=pltpu.PrefetchScalarGridSpec(
            num_scalar_prefetch=2,
            grid=(n // tm,),
            in_specs=[pl.BlockSpec((tm, d), lambda i, dest, pend: (i, 0))],
            out_specs=pl.BlockSpec(memory_space=pl.ANY),
            scratch_shapes=[pltpu.VMEM((MOE_BM, d), F32), pltpu.SemaphoreType.DMA(()),
                            pltpu.SemaphoreType.DMA(())],
        ),
        out_shape=jax.ShapeDtypeStruct((cap, d), F32),
        compiler_params=_params("arbitrary"),
        name="moe_dispatch",
    )(dest, pend, h)


def _experts_kernel(be_ref, nu_ref, x_ref, w1_ref, w3_ref, w2_ref, o_ref, xb_ref, acc_ref):
    i, f = pl.program_id(0), pl.program_id(1)
    used = i < nu_ref[0]

    @pl.when(jnp.logical_and(used, f == 0))
    def _():
        xb_ref[...] = x_ref[...].astype(BF16)

    @pl.when(used)
    def _():
        x = xb_ref[...]
        h1 = _dot(x, w1_ref[0])
        act = (h1 * _sigmoid(h1) * _dot(x, w3_ref[0])).astype(BF16)
        part = _dot(act, w2_ref[0])

        @pl.when(f == 0)
        def _():
            acc_ref[...] = part

        @pl.when(f > 0)
        def _():
            acc_ref[...] += part

    @pl.when(f == pl.num_programs(1) - 1)
    def _():
        o_ref[...] = jnp.where(used, acc_ref[...], 0.0)


def _experts(xs, blk_e, n_used, w1, w3, w2, tf=MOE_TF):
    cap, d = xs.shape
    edim = w1.shape[2]
    nb = cap // MOE_BM
    return pl.pallas_call(
        _experts_kernel,
        grid_spec=pltpu.PrefetchScalarGridSpec(
            num_scalar_prefetch=2,
            grid=(nb, edim // tf),
            in_specs=[pl.BlockSpec((MOE_BM, d), lambda i, f, be, nu: (jnp.minimum(i, nu[0] - 1), 0)),
                      pl.BlockSpec((1, d, tf), lambda i, f, be, nu: (be[i], 0, f)),
                      pl.BlockSpec((1, d, tf), lambda i, f, be, nu: (be[i], 0, f)),
                      pl.BlockSpec((1, tf, d), lambda i, f, be, nu: (be[i], f, 0))],
            out_specs=pl.BlockSpec((MOE_BM, d), lambda i, f, be, nu: (i, 0)),
            scratch_shapes=[pltpu.VMEM((MOE_BM, d), BF16), pltpu.VMEM((MOE_BM, d), F32)],
        ),
        out_shape=jax.ShapeDtypeStruct((cap, d), F32),
        compiler_params=_params("arbitrary", "arbitrary"),
        name="moe_experts",
    )(blk_e, n_used, xs, w1.astype(BF16), w3.astype(BF16), w2.astype(BF16))


def _combine_kernel(dest_ref, x_ref, route_ref, yb_ref, o_ref, buf, sem):
    i = pl.program_id(0)
    nt = pl.num_programs(0)
    tm = x_ref.shape[0]
    n = nt * tm

    def row_copy(tile, slot, j, k):
        return pltpu.make_async_copy(yb_ref.at[pl.ds(dest_ref[k * n + tile * tm + j], 1)],
                                     buf.at[slot, k, pl.ds(j, 1)], sem.at[slot])

    def fetch(tile, slot):
        def body(j, c):
            row_copy(tile, slot, j, 0).start()
            row_copy(tile, slot, j, 1).start()
            return c
        lax.fori_loop(0, tm, body, 0, unroll=8)

    def drain(tile, slot):
        def body(j, c):
            row_copy(tile, slot, j, 0).wait()
            row_copy(tile, slot, j, 1).wait()
            return c
        lax.fori_loop(0, tm, body, 0, unroll=8)

    slot = i % 2

    @pl.when(i == 0)
    def _():
        fetch(0, 0)

    @pl.when(i + 1 < nt)
    def _():
        fetch(i + 1, 1 - slot)

    drain(i, slot)
    p1 = route_ref[:, 2:3]
    p2 = route_ref[:, 3:4]
    o_ref[...] = x_ref[...] + (p1 * buf[slot, 0] + p2 * buf[slot, 1])


def _combine(x, route, yb, dest, tm=512):
    n, d = x.shape
    return pl.pallas_call(
        _combine_kernel,
        grid_spec=pltpu.PrefetchScalarGridSpec(
            num_scalar_prefetch=1,
            grid=(n // tm,),
            in_specs=[pl.BlockSpec((tm, d), lambda i, dest: (i, 0)),
                      pl.BlockSpec((tm, LANES), lambda i, dest: (i, 0)),
                      pl.BlockSpec(memory_space=pl.ANY)],
            out_specs=pl.BlockSpec((tm, d), lambda i, dest: (i, 0)),
            scratch_shapes=[pltpu.VMEM((2, 2, tm, d), F32), pltpu.SemaphoreType.DMA((2,))],
        ),
        out_shape=jax.ShapeDtypeStruct((n, d), F32),
        compiler_params=_params("arbitrary"),
        name="moe_combine",
    )(dest, x, route, yb)


def _moe(x, gn, w_router, w1, w3, w2):
    n, d = x.shape
    h, route, cnt = _router(x, gn, w_router)
    counts = cnt[0, :N_EXPERTS].astype(jnp.int32)
    padded = ((counts + MOE_BM - 1) // MOE_BM) * MOE_BM
    pend = jnp.cumsum(padded).astype(jnp.int32)
    pstart = pend - padded
    experts = jnp.arange(N_EXPERTS, dtype=jnp.int32)

    def dest_of(e_col, r_col):
        e = route[:, e_col].astype(jnp.int32)
        start = jnp.sum(jnp.where(e[:, None] == experts[None, :], pstart[None, :], 0), axis=1)
        return start + route[:, r_col].astype(jnp.int32)

    dest = jnp.concatenate([dest_of(0, 4), dest_of(1, 5)])
    nb = -(-(2 * n) // MOE_BM) + N_EXPERTS
    blk_e = jnp.minimum(jnp.searchsorted(pend, jnp.arange(nb, dtype=jnp.int32) * MOE_BM, side="right"),
                        N_EXPERTS - 1).astype(jnp.int32)
    n_used = (pend[-1:] // MOE_BM).astype(jnp.int32)
    xs = _dispatch(h, dest, pend, nb * MOE_BM)
    yb = _experts(xs, blk_e, n_used, w1, w3, w2)
    return _combine(x, route, yb, dest)


def _kv_kernel(x_ref, gn_ref, wk_ref, wv_ref, wf_ref, bf_ref, kn_ref, k_out, v_out, dp_out, carry):
    s = pl.program_id(1)
    tm = x_ref.shape[1]

    @pl.when(s == 0)
    def _():
        carry[...] = jnp.zeros_like(carry)

    h = _rms(x_ref[0], gn_ref[...])
    hb = h.astype(BF16)
    k_out[0] = _head_rms(_dot(hb, wk_ref[...]), kn_ref[...]).astype(BF16)
    v_out[0] = _dot(hb, wv_ref[...]).astype(BF16)
    fl = _dot(hb, wf_ref[...]) + bf_ref[...]
    log_f = jnp.minimum(fl, 0.0) - jnp.log(1.0 + jnp.exp(-jnp.abs(fl)))
    tri = (lax.broadcasted_iota(jnp.int32, (tm, tm), 0) >= lax.broadcasted_iota(jnp.int32, (tm, tm), 1)).astype(F32)
    dcum = carry[...] + _dot_hi(tri, log_f)
    carry[...] = dcum[tm - 1:tm, :]
    rest = -dcum * LOG2_E
    pieces = []
    for _ in range(N_BIAS):
        piece = rest.astype(BF16)
        pieces.append(piece)
        rest = rest - piece.astype(F32)
    dp_out[0] = jnp.concatenate(pieces, axis=1)


def _shared_kv(x, gn, w_kvf, b_f, k_norm, tm=256):
    B, S, D = x.shape
    wk = w_kvf[:, :D].astype(BF16)
    wv = w_kvf[:, D:2 * D].astype(BF16)
    wf = jnp.zeros((D, LANES), F32).at[:, :N_HEADS].set(w_kvf[:, 2 * D:]).astype(BF16)
    bf = jnp.zeros((1, LANES), F32).at[0, :N_HEADS].set(b_f)
    kn = jnp.tile(k_norm, N_HEADS).reshape(1, D)
    tok = pl.BlockSpec((1, tm, D), lambda b, s: (b, s, 0))
    full = lambda a: pl.BlockSpec(a.shape, lambda b, s: (0,) * a.ndim)
    ins = [x, gn.reshape(1, D), wk, wv, wf, bf, kn]
    return pl.pallas_call(
        _kv_kernel,
        grid=(B, S // tm),
        in_specs=[tok] + [full(a) for a in ins[1:]],
        out_specs=[tok, tok, pl.BlockSpec((1, tm, N_BIAS * LANES), lambda b, s: (b, s, 0))],
        out_shape=[jax.ShapeDtypeStruct((B, S, D), BF16), jax.ShapeDtypeStruct((B, S, D), BF16),
                   jax.ShapeDtypeStruct((B, S, N_BIAS * LANES), BF16)],
        scratch_shapes=[pltpu.VMEM((1, LANES), F32)],
        compiler_params=_params("arbitrary", "arbitrary"),
        name="shared_kv",
    )(*ins)


def _qg_kernel(x_ref, gn_ref, wq_ref, wg_ref, qn_ref, q_out, og_out):
    hb = _rms(x_ref[...], gn_ref[...]).astype(BF16)
    q = _head_rms(_dot(hb, wq_ref[...]), qn_ref[...])
    q_out[...] = (q * (HEAD_DIM ** -0.5 * LOG2_E)).astype(BF16)
    og_out[...] = _dot(hb, wg_ref[...]).astype(BF16)


def _qg_proj(x, gn, w_qg, q_norm, tm=512):
    n, d = x.shape
    qn = jnp.tile(q_norm, N_HEADS).reshape(1, d)
    tok = pl.BlockSpec((tm, d), lambda i: (i, 0))
    const = lambda a: pl.BlockSpec(a.shape, lambda i: (0, 0))
    ins = [x, gn.reshape(1, d), w_qg[:, :d].astype(BF16), w_qg[:, d:].astype(BF16), qn]
    return pl.pallas_call(
        _qg_kernel,
        grid=(n // tm,),
        in_specs=[tok] + [const(a) for a in ins[1:]],
        out_specs=[tok, tok],
        out_shape=[jax.ShapeDtypeStruct((n, d), BF16)] * 2,
        compiler_params=_params("arbitrary"),
        name="fox_qg",
    )(*ins)


N_BIAS = 3


def _attn_kernel(qt_ref, k_ref, vt_ref, o_ref):
    tq, tk = ATT_TQ, ATT_TK
    i = pl.program_id(2)
    hs = range(ATT_HEADS)
    qt = [qt_ref[0, h] for h in hs]
    q_pos = i * tq + lax.broadcasted_iota(jnp.int32, (tk, tq), 1)
    k_off = lax.broadcasted_iota(jnp.int32, (tk, tq), 0)

    def step(j, carry, masked):
        off = pl.multiple_of(j * tk, tk)
        s = [_dot(k_ref[0, h, pl.ds(off, tk), :], qt[h]) for h in hs]
        if masked:
            keep = k_off + off <= q_pos
            s = [jnp.where(keep, x, -jnp.inf) for x in s]
        m_new = [jnp.maximum(carry[h][0], jnp.max(s[h], axis=0, keepdims=True)) for h in hs]
        alpha = [jnp.exp2(carry[h][0] - m_new[h]) for h in hs]
        p = [jnp.exp2(s[h] - m_new[h]) for h in hs]
        l = [alpha[h] * carry[h][1] + jnp.sum(p[h], axis=0, keepdims=True) for h in hs]
        pv = [_dot(vt_ref[0, h, :, pl.ds(off, tk)], p[h].astype(BF16)) for h in hs]
        return tuple((m_new[h], l[h], alpha[h] * carry[h][2] + pv[h]) for h in hs)

    init = tuple((jnp.full((1, tq), -jnp.inf, F32), jnp.zeros((1, tq), F32), jnp.zeros((HEAD_DIM, tq), F32))
                 for _ in hs)
    carry = lax.fori_loop(0, i, lambda j, c: step(j, c, False), init)
    carry = step(i, carry, True)
    for h in hs:
        _, l, acc = carry[h]
        o_ref[0, h] = (acc / l).astype(BF16)


def _fox_attn(qt, k_aug, vt):
    B, H, _, S = qt.shape
    g = ATT_HEADS
    return pl.pallas_call(
        _attn_kernel,
        grid=(B, H // g, S // ATT_TQ),
        in_specs=[pl.BlockSpec((1, g, LANES, ATT_TQ), lambda b, h, i: (b, h, 0, i)),
                  pl.BlockSpec((1, g, S, LANES), lambda b, h, i: (b, h, 0, 0)),
                  pl.BlockSpec((1, g, HEAD_DIM, S), lambda b, h, i: (b, h, 0, 0))],
        out_specs=pl.BlockSpec((1, g, HEAD_DIM, ATT_TQ), lambda b, h, i: (b, h, 0, i)),
        out_shape=jax.ShapeDtypeStruct((B, H, HEAD_DIM, S), BF16),
        compiler_params=_params("arbitrary", "arbitrary", "arbitrary"),
        name="fox_attn",
    )(qt, k_aug, vt)


def _split_heads_t(a, B, S):
    return a.reshape(B, S, N_HEADS, HEAD_DIM).transpose(0, 2, 3, 1)


def _attn_kv_layout(k_sh, v_sh, d_pieces):
    B, S, _ = k_sh.shape
    kh = k_sh.reshape(B, S, N_HEADS, HEAD_DIM).transpose(0, 2, 1, 3)
    dp = d_pieces.reshape(B, S, N_BIAS, LANES)[..., :N_HEADS].transpose(0, 3, 1, 2)
    pad = jnp.zeros((B, N_HEADS, S, LANES - HEAD_DIM - N_BIAS), BF16)
    return jnp.concatenate([kh, dp, pad], axis=-1), _split_heads_t(v_sh, B, S)


def _attn_q_layout(q, B, S):
    qt = _split_heads_t(q, B, S)
    ones = jnp.ones((B, N_HEADS, N_BIAS, S), BF16)
    pad = jnp.zeros((B, N_HEADS, LANES - HEAD_DIM - N_BIAS, S), BF16)
    return jnp.concatenate([qt, ones, pad], axis=2)


def kernel(x, norm_mix, norm_ffn, rk_mu, rk_w_r, rk_w_k, rk_w_v, rk_w_o, rk_w0, rk_w1, rk_w2, rk_a0, rk_a1, rk_a2, rk_v0, rk_v1, rk_v2, rk_g1, rk_g2, rk_k_k, rk_k_a, rk_r_k, rk_lnx_w, rk_lnx_b, kv_norm, w_kvf, b_f, k_norm, fx_w_qg, fx_q_norm, fx_w_o, ffn_w1, ffn_w3, ffn_w2, moe_router, moe_w1, moe_w3, moe_w2):
    B, S, D = x.shape
    n = B * S
    depth = norm_mix.shape[0]
    n_rwkv = rk_mu.shape[0]
    v_first = None
    kv = None
    for l in range(depth):
        if l < n_rwkv:
            vres = None if l == 0 else (rk_v0[l - 1], rk_v1[l - 1], rk_v2[l - 1])
            r, lw, k, v, a, g = _rwkv_proj(
                x, norm_mix[l], rk_mu[l], rk_w_r[l], rk_w_k[l], rk_w_v[l], rk_w0[l], rk_w1[l], rk_w2[l],
                rk_a0[l], rk_a1[l], rk_a2[l], rk_g1[l], rk_g2[l], vres, v_first)
            if l == 0:
                v_first = v
            z = _wkv(r, lw, k, v, a, g, rk_k_k[l], rk_k_a[l], rk_r_k[l], rk_lnx_w[l], rk_lnx_b[l])
            x2 = _proj_res(z.reshape(n, D), rk_w_o[l], x.reshape(n, D))
        else:
            if kv is None:
                k_sh, v_sh, d_pieces = _shared_kv(x, kv_norm, w_kvf, b_f, k_norm)
                kv = _attn_kv_layout(k_sh, v_sh, d_pieces)
            j = l - n_rwkv
            q, og = _qg_proj(x.reshape(n, D), norm_mix[l], fx_w_qg[j], fx_q_norm[j])
            ot = _fox_attn(_attn_q_layout(q, B, S), *kv)
            o = ot.transpose(0, 3, 1, 2).reshape(n, D)
            x2 = _proj_res(o, fx_w_o[j], x.reshape(n, D), gate=og)
        i = l // 2
        if l % 2 == 0:
            x2 = _ffn(x2, norm_ffn[l], ffn_w1[i], ffn_w3[i], ffn_w2[i])
        else:
            x2 = _moe(x2, norm_ffn[l], moe_router[i], moe_w1[i], moe_w3[i], moe_w2[i])
        x = x2.reshape(B, S, D)
    return x
```

```python
import functools

import jax
import jax.numpy as jnp
from jax import lax
from jax.experimental import pallas as pl
from jax.experimental.pallas import tpu as pltpu

F32 = jnp.float32
BF16 = jnp.bfloat16
HI = lax.Precision.HIGHEST

D_MODEL = 1024
HEAD_DIM = 64
N_HEADS = D_MODEL // HEAD_DIM
LANES = 128
HEAD_PAIRS = D_MODEL // LANES
N_EXPERTS = 8
LOG2_E = 1.4426950408889634
N_BIAS = 3
RMS_EPS = 1e-6
GN_EPS = 64e-5
VMEM_LIMIT_BYTES = 56 * 1024 * 1024

WKV_CHUNK = 64
WKV_TILE = 512
WKV_PAIRS = 2
ATT_TQ = 512
ATT_TK = 512
ATT_HEADS = 4
MOE_BM = 1024
MOE_TF = 512


def _params(*sem):
    return pltpu.CompilerParams(dimension_semantics=sem, vmem_limit_bytes=VMEM_LIMIT_BYTES)


def _dot(a, b):
    return jnp.dot(a, b, preferred_element_type=F32)


def _dot_hi(a, b):
    return jnp.dot(a, b, preferred_element_type=F32, precision=HI)


def _dot_nt(a, b):
    return lax.dot_general(a, b, (((1,), (1,)), ((), ())), preferred_element_type=F32)


def _dot_tn(a, b):
    return lax.dot_general(a, b, (((0,), (0,)), ((), ())), preferred_element_type=F32)


def _rms(x, g):
    return x * lax.rsqrt(jnp.mean(x * x, axis=-1, keepdims=True) + RMS_EPS) * g


def _sigmoid(x):
    return 1.0 / (1.0 + jnp.exp(-x))


def _pair_ones():
    ri = lax.broadcasted_iota(jnp.int32, (LANES, LANES), 0)
    ci = lax.broadcasted_iota(jnp.int32, (LANES, LANES), 1)
    return ((ri < HEAD_DIM) == (ci < HEAD_DIM)).astype(F32)


def _head_rms(x, gain_row):
    ones = _pair_ones()
    parts = []
    for p in range(HEAD_PAIRS):
        xs = x[:, p * LANES:(p + 1) * LANES]
        ms = _dot_hi(xs * xs, ones) * (1.0 / HEAD_DIM)
        parts.append(xs * lax.rsqrt(ms + RMS_EPS))
    return jnp.concatenate(parts, axis=1) * gain_row


def _rwkv_proj_kernel(has_vres, *refs):
    if has_vres:
        (x_ref, gn_ref, mu_ref, wr_ref, wk_ref, wv_ref, w0_ref, w1_ref, w2_ref, a0_ref, a1_ref,
         a2_ref, g1_ref, g2_ref, v0_ref, v1_ref, v2_ref, vf_ref,
         r_out, lw_out, k_out, v_out, a_out, g_out, carry) = refs
    else:
        (x_ref, gn_ref, mu_ref, wr_ref, wk_ref, wv_ref, w0_ref, w1_ref, w2_ref, a0_ref, a1_ref,
         a2_ref, g1_ref, g2_ref,
         r_out, lw_out, k_out, v_out, a_out, g_out, carry) = refs
    s = pl.program_id(1)
    tm = x_ref.shape[1]

    @pl.when(s == 0)
    def _():
        carry[...] = jnp.zeros_like(carry)

    h = _rms(x_ref[0], gn_ref[...])
    prev_last = carry[...]
    carry[...] = h[tm - 1:tm, :]
    row = lax.broadcasted_iota(jnp.int32, (tm, 1), 0)
    xx = jnp.where(row == 0, prev_last, pltpu.roll(h, 1, 0)) - h

    def mix(i):
        return (h + xx * mu_ref[i:i + 1, :]).astype(BF16)

    xr, xw, xk, xv, xa, xg = [mix(i) for i in range(6)]
    r_out[0] = _dot(xr, wr_ref[...]).astype(BF16)
    k_out[0] = _dot(xk, wk_ref[...]).astype(BF16)
    v = _dot(xv, wv_ref[...])
    wl = w0_ref[...] + _dot(jnp.tanh(_dot(xw, w1_ref[...])).astype(BF16), w2_ref[...])
    w = -(jnp.maximum(-wl, 0.0) + jnp.log(1.0 + jnp.exp(-jnp.abs(wl)))) - 0.5
    lw_out[0] = -jnp.exp(w)
    a_out[0] = _sigmoid(a0_ref[...] + _dot(_dot(xa, a1_ref[...]).astype(BF16), a2_ref[...])).astype(BF16)
    g_out[0] = _dot(_sigmoid(_dot(xg, g1_ref[...])).astype(BF16), g2_ref[...]).astype(BF16)
    if has_vres:
        gate = _sigmoid(v0_ref[...] + _dot(_dot(xv, v1_ref[...]).astype(BF16), v2_ref[...]))
        v = v + (vf_ref[0].astype(F32) - v) * gate
    v_out[0] = v.astype(BF16)


def _rwkv_proj(x, gn, mu, wr, wk, wv, w0, w1, w2, a0, a1, a2, g1, g2, vres, v_first, tm=512):
    B, S, D = x.shape
    has_vres = vres is not None
    row = lambda a: a.reshape(1, -1)
    full = lambda a: pl.BlockSpec(a.shape, lambda b, s: (0,) * a.ndim, pipeline_mode=pl.Buffered(1))
    tok = pl.BlockSpec((1, tm, D), lambda b, s: (b, s, 0))
    ins = [x, row(gn), mu, wr.astype(BF16), wk.astype(BF16), wv.astype(BF16), row(w0),
           w1.astype(BF16), w2.astype(BF16), row(a0), a1.astype(BF16), a2.astype(BF16),
           g1.astype(BF16), g2.astype(BF16)]
    if has_vres:
        v0, v1, v2 = vres
        ins += [row(v0), v1.astype(BF16), v2.astype(BF16)]
    specs = [tok] + [full(a) for a in ins[1:]]
    if has_vres:
        ins.append(v_first)
        specs.append(tok)
    out_shape = [jax.ShapeDtypeStruct((B, S, D), dt) for dt in (BF16, F32, BF16, BF16, BF16, BF16)]
    return pl.pallas_call(
        functools.partial(_rwkv_proj_kernel, has_vres),
        grid=(B, S // tm),
        in_specs=specs,
        out_specs=[tok] * 6,
        out_shape=out_shape,
        scratch_shapes=[pltpu.VMEM((1, D), F32)],
        compiler_params=_params("arbitrary", "arbitrary"),
        name="rwkv_proj_vres" if has_vres else "rwkv_proj",
    )(*ins)


def _head_sum(x, ones_b):
    hi = x.astype(BF16)
    lo = (x - hi.astype(F32)).astype(BF16)
    return _dot(hi, ones_b) + _dot(lo, ones_b)


def _wkv_kernel(r_ref, lw_ref, k_ref, v_ref, a_ref, g_ref, kk_ref, ka_ref, rk_ref, lnw_ref, lnb_ref,
                z_ref, state):
    C = WKV_CHUNK
    P = 2 * C
    tt = r_ref.shape[1]
    nc = tt // C

    @pl.when(pl.program_id(2) == 0)
    def _():
        state[...] = jnp.zeros_like(state)

    lane = lax.broadcasted_iota(jnp.int32, (C, LANES), 1)
    m0 = (lane < HEAD_DIM).astype(F32)
    m1 = 1.0 - m0
    ri = lax.broadcasted_iota(jnp.int32, (P, P), 0)
    ci = lax.broadcasted_iota(jnp.int32, (P, P), 1)
    strict = ri > ci
    incl = ri >= ci
    eye = (ri == ci).astype(F32)
    tri = (lax.broadcasted_iota(jnp.int32, (C, C), 0) >= lax.broadcasted_iota(jnp.int32, (C, C), 1)).astype(F32)
    ones_b = _pair_ones().astype(BF16)

    def stack(x):
        return jnp.concatenate([x * m0, x * m1], axis=0).astype(BF16)

    nq = r_ref.shape[2] // LANES
    cs = range(nc)
    pre = []
    for q in range(nq):
        ls = slice(q * LANES, (q + 1) * LANES)
        r = r_ref[0, :, ls].astype(F32)
        k = k_ref[0, :, ls].astype(F32)
        v = v_ref[0, :, ls].astype(F32)
        a = a_ref[0, :, ls].astype(F32)
        lw = lw_ref[0, :, ls]
        kk = k * kk_ref[:, ls]
        kk = kk * lax.rsqrt(jnp.maximum(_head_sum(kk * kk, ones_b), 1e-24))
        k2 = k * (1.0 + (a - 1.0) * ka_ref[:, ls])
        bv = kk * a
        cum_c = _dot_hi(tri, jnp.concatenate([lw[c * C:(c + 1) * C] for c in cs], axis=1))
        cum = jnp.concatenate([cum_c[:, c * LANES:(c + 1) * LANES] for c in cs], axis=0)
        cum_end = [cum_c[C - 1:C, c * LANES:(c + 1) * LANES] for c in cs]
        to_end = jnp.concatenate([jnp.broadcast_to(ce, (C, LANES)) for ce in cum_end], axis=0) - cum
        g_inv = jnp.exp(-cum)
        g_end = jnp.exp(to_end)
        pre.append(dict(r=r, k2=k2, v=v, cum_end=cum_end, a_t=-kk * jnp.exp(cum - lw), r_t=r * jnp.exp(cum),
                        b_t=bv * g_inv, k_t=k2 * g_inv, b_h=bv * g_end, k_h=k2 * g_end))

    units = [(q, c) for c in cs for q in range(nq)]
    us = range(len(units))

    def stacks(name):
        return [stack(pre[q][name][c * C:(c + 1) * C]) for q, c in units]

    a_s, r_s, b_s, k_s, v_s, bh_s, kh_s = (stacks(x) for x in ("a_t", "r_t", "b_t", "k_t", "v", "b_h", "k_h"))
    cat0 = lambda x, y: jnp.concatenate([x, y], axis=0)
    cat1 = lambda x, y: jnp.concatenate([x, y], axis=1)
    gram = [_dot_nt(cat0(a_s[u], r_s[u]), cat0(b_s[u], k_s[u])) for u in us]
    n_ab = [jnp.where(strict, g[:P, :P], 0.0) for g in gram]
    a_ak = [jnp.where(strict, g[:P, P:], 0.0).astype(BF16) for g in gram]
    a_rb = [jnp.where(incl, g[P:, :P], 0.0).astype(BF16) for g in gram]
    a_rk = [jnp.where(incl, g[P:, P:], 0.0).astype(BF16) for g in gram]
    akv = [_dot(a_ak[u], v_s[u]).astype(BF16) for u in us]
    pw = [n.astype(BF16) for n in n_ab]
    inv = [eye + n for n in n_ab]
    pw = [_dot(p, p).astype(BF16) for p in pw]
    for _ in range(4):
        both = [_dot(cat0(pw[u], inv[u].astype(BF16)), pw[u]) for u in us]
        pw = [x[:P].astype(BF16) for x in both]
        inv = [inv[u] + both[u][P:] for u in us]
    inv = [(inv[u] + _dot(inv[u].astype(BF16), pw[u])).astype(BF16) for u in us]
    wu = [_dot(inv[u], cat1(a_s[u], akv[u])).astype(BF16) for u in us]
    rb = [_dot(a_rb[u], wu[u]) for u in us]
    q_h = [(r_s[u].astype(F32) + rb[u][:, :LANES]).astype(BF16) for u in us]
    y0 = [rb[u][:, LANES:] + _dot(a_rk[u], v_s[u]) for u in us]
    m_st = [_dot_tn(wu[u][:, :LANES], bh_s[u]).astype(BF16) for u in us]
    z_st = [_dot_tn(cat0(wu[u][:, LANES:], v_s[u]), cat0(bh_s[u], kh_s[u])) for u in us]

    s = [state[q] for q in range(nq)]
    ys = [[] for _ in range(nq)]
    for c in cs:
        for q in range(nq):
            u = c * nq + q
            sb = s[q].astype(BF16)
            y_s = _dot_nt(q_h[u], sb) + y0[u]
            ys[q].append(y_s[:C] + y_s[C:])
            s[q] = s[q] * jnp.exp(pre[q]["cum_end"][c]) + _dot(sb, m_st[u]) + z_st[u]
    for q in range(nq):
        state[q] = s[q]

    for q in range(nq):
        ls = slice(q * LANES, (q + 1) * LANES)
        y = jnp.concatenate(ys[q], axis=0)
        mean = _head_sum(y, ones_b) * (1.0 / HEAD_DIM)
        yc = y - mean
        var = _head_sum(yc * yc, ones_b) * (1.0 / HEAD_DIM)
        yn = yc * lax.rsqrt(var + GN_EPS) * lnw_ref[:, ls] + lnb_ref[:, ls]
        bonus = _head_sum(pre[q]["r"] * pre[q]["k2"] * rk_ref[:, ls], ones_b) * pre[q]["v"]
        z_ref[0, :, ls] = ((yn + bonus) * g_ref[0, :, ls].astype(F32)).astype(BF16)


def _wkv(r, lw, k, v, a, g, k_k, k_a, r_k, ln_w, ln_b):
    B, S, D = r.shape
    tt = min(WKV_TILE, S)
    width = WKV_PAIRS * LANES
    tok = pl.BlockSpec((1, tt, width), lambda b, p, t: (b, t, p))
    par = pl.BlockSpec((1, width), lambda b, p, t: (0, p))
    row = lambda x: x.reshape(1, D)
    return pl.pallas_call(
        _wkv_kernel,
        grid=(B, D // width, S // tt),
        in_specs=[tok] * 6 + [par] * 5,
        out_specs=tok,
        out_shape=jax.ShapeDtypeStruct((B, S, D), BF16),
        scratch_shapes=[pltpu.VMEM((WKV_PAIRS, LANES, LANES), F32)],
        compiler_params=_params("arbitrary", "arbitrary", "arbitrary"),
        name="wkv7",
    )(r, lw, k, v, a, g, row(k_k), row(k_a), row(r_k), row(ln_w), row(ln_b))


def _proj_res_kernel(a_ref, w_ref, res_ref, o_ref):
    o_ref[...] = res_ref[...] + _dot(a_ref[...], w_ref[...])


def _proj_res(a, w, res, tm=512):
    n, kdim = a.shape
    d = w.shape[1]
    return pl.pallas_call(
        _proj_res_kernel,
        grid=(n // tm,),
        in_specs=[pl.BlockSpec((tm, kdim), lambda i: (i, 0)), pl.BlockSpec((kdim, d), lambda i: (0, 0)),
                  pl.BlockSpec((tm, d), lambda i: (i, 0))],
        out_specs=pl.BlockSpec((tm, d), lambda i: (i, 0)),
        out_shape=jax.ShapeDtypeStruct((n, d), F32),
        compiler_params=_params("arbitrary"),
        name="proj_res",
    )(a, w.astype(BF16), res)


def _ffn_kernel(x_ref, gn_ref, w1_ref, w3_ref, w2_ref, o_ref, h_ref, acc_ref):
    f = pl.program_id(1)

    @pl.when(f == 0)
    def _():
        h_ref[...] = _rms(x_ref[...], gn_ref[...]).astype(BF16)

    h = h_ref[...]
    h1 = _dot(h, w1_ref[...])
    act = (h1 * _sigmoid(h1) * _dot(h, w3_ref[...])).astype(BF16)
    part = _dot(act, w2_ref[...])

    @pl.when(f == 0)
    def _():
        acc_ref[...] = part

    @pl.when(f > 0)
    def _():
        acc_ref[...] += part

    @pl.when(f == pl.num_programs(1) - 1)
    def _():
        o_ref[...] = x_ref[...] + acc_ref[...]


def _ffn(x, gn, w1, w3, w2, tm=512, tf=1408):
    n, d = x.shape
    fdim = w1.shape[1]
    return pl.pallas_call(
        _ffn_kernel,
        grid=(n // tm, fdim // tf),
        in_specs=[pl.BlockSpec((tm, d), lambda i, f: (i, 0)),
                  pl.BlockSpec((1, d), lambda i, f: (0, 0)),
                  pl.BlockSpec((d, tf), lambda i, f: (0, f)),
                  pl.BlockSpec((d, tf), lambda i, f: (0, f)),
                  pl.BlockSpec((tf, d), lambda i, f: (f, 0))],
        out_specs=pl.BlockSpec((tm, d), lambda i, f: (i, 0)),
        out_shape=jax.ShapeDtypeStruct((n, d), F32),
        scratch_shapes=[pltpu.VMEM((tm, d), BF16), pltpu.VMEM((tm, d), F32)],
        compiler_params=_params("arbitrary", "arbitrary"),
        name="ffn_dense",
    )(x, gn.reshape(1, d), w1.astype(BF16), w3.astype(BF16), w2.astype(BF16))


def _router_kernel(x_ref, gn_ref, wr_ref, h_ref, route_ref, cnt_ref, carry):
    tm = x_ref.shape[0]

    @pl.when(pl.program_id(0) == 0)
    def _():
        carry[...] = jnp.zeros_like(carry)

    h = _rms(x_ref[...], gn_ref[...])
    h_ref[...] = h
    logits = _dot_hi(h, wr_ref[...])
    lane = lax.broadcasted_iota(jnp.int32, logits.shape, 1)
    neg = jnp.float32(-jnp.inf)
    logits = jnp.where(lane < N_EXPERTS, logits, neg)
    m1 = jnp.max(logits, axis=-1, keepdims=True)
    i1 = jnp.min(jnp.where(logits == m1, lane, LANES), axis=-1, keepdims=True)
    rest = jnp.where(lane == i1, neg, logits)
    m2 = jnp.max(rest, axis=-1, keepdims=True)
    i2 = jnp.min(jnp.where(rest == m2, lane, LANES), axis=-1, keepdims=True)
    e2 = jnp.exp(m2 - m1)
    p1 = 1.0 / (1.0 + e2)
    p2 = e2 * p1
    oh1 = (lane == i1).astype(F32)
    oh2 = (lane == i2).astype(F32)
    cnt = oh1 + oh2
    earlier = (lax.broadcasted_iota(jnp.int32, (tm, tm), 0) > lax.broadcasted_iota(jnp.int32, (tm, tm), 1))
    before = carry[...] + _dot(earlier.astype(BF16), cnt.astype(BF16))
    r1 = jnp.sum(oh1 * before, axis=-1, keepdims=True)
    r2 = jnp.sum(oh2 * before, axis=-1, keepdims=True)
    total = carry[...] + jnp.sum(cnt, axis=0, keepdims=True)
    carry[...] = total
    cnt_ref[...] = total
    cols = (i1.astype(F32), i2.astype(F32), p1, p2, r1, r2)
    route = jnp.zeros(logits.shape, F32)
    for c, val in enumerate(cols):
        route = jnp.where(lane == c, val, route)
    route_ref[...] = route


def _router(x, gn, w_router, tm=512):
    n, d = x.shape
    wr = jnp.zeros((d, LANES), F32).at[:, :N_EXPERTS].set(w_router)
    return pl.pallas_call(
        _router_kernel,
        grid=(n // tm,),
        in_specs=[pl.BlockSpec((tm, d), lambda i: (i, 0)),
                  pl.BlockSpec((1, d), lambda i: (0, 0)),
                  pl.BlockSpec((d, LANES), lambda i: (0, 0))],
        out_specs=[pl.BlockSpec((tm, d), lambda i: (i, 0)), pl.BlockSpec((tm, LANES), lambda i: (i, 0)),
                   pl.BlockSpec((1, LANES), lambda i: (0, 0))],
        out_shape=[jax.ShapeDtypeStruct((n, d), F32), jax.ShapeDtypeStruct((n, LANES), F32),
                   jax.ShapeDtypeStruct((1, LANES), F32)],
        scratch_shapes=[pltpu.VMEM((1, LANES), F32)],
        compiler_params=_params("arbitrary"),
        name="moe_router",
    )(x, gn.reshape(1, d), wr)


def _dispatch_kernel(dest_ref, pend_ref, h_ref, xs_ref, zeros, sem, zsem):
    i = pl.program_id(0)
    tm = h_ref.shape[0]
    n = pl.num_programs(0) * tm

    @pl.when(i == 0)
    def _():
        zeros[...] = jnp.zeros_like(zeros)

        def tail(e):
            start = pl.multiple_of(pend_ref[e] - MOE_BM, MOE_BM)
            return pltpu.make_async_copy(zeros, xs_ref.at[pl.ds(start, MOE_BM)], zsem)

        def nonempty(e):
            return pend_ref[e] > (pend_ref[e - 1] if e else 0)

        for e in range(N_EXPERTS):
            pl.when(nonempty(e))(lambda e=e: tail(e).start())
        for e in range(N_EXPERTS):
            pl.when(nonempty(e))(lambda e=e: tail(e).wait())

        def spare(b):
            return pltpu.make_async_copy(zeros, xs_ref.at[pl.ds(pl.multiple_of(b * MOE_BM, MOE_BM), MOE_BM)], zsem)

        first_spare = pend_ref[N_EXPERTS - 1] // MOE_BM
        n_blocks = xs_ref.shape[0] // MOE_BM
        lax.fori_loop(first_spare, n_blocks, lambda b, c: (spare(b).start(), c)[1], 0)
        lax.fori_loop(first_spare, n_blocks, lambda b, c: (spare(b).wait(), c)[1], 0)

    def row_copy(j, k):
        return pltpu.make_async_copy(h_ref.at[pl.ds(j, 1)],
                                     xs_ref.at[pl.ds(dest_ref[k * n + i * tm + j], 1)], sem)

    def start(j, c):
        row_copy(j, 0).start()
        row_copy(j, 1).start()
        return c

    def wait(j, c):
        row_copy(j, 0).wait()
        row_copy(j, 1).wait()
        return c

    lax.fori_loop(0, tm, start, 0, unroll=8)
    lax.fori_loop(0, tm, wait, 0, unroll=8)


def _dispatch(h, dest, pend, cap, tm=512):
    n, d = h.shape
    return pl.pallas_call(
        _dispatch_kernel,
        grid_spec=pltpu.PrefetchScalarGridSpec(
            num_scalar_prefetch=2,
            grid=(n // tm,),
            in_specs=[pl.BlockSpec((tm, d), lambda i, dest, pend: (i, 0))],
            out_specs=pl.BlockSpec(memory_space=pl.ANY),
            scratch_shapes=[pltpu.VMEM((MOE_BM, d), F32), pltpu.SemaphoreType.DMA(()),
                            pltpu.SemaphoreType.DMA(())],
        ),
        out_shape=jax.ShapeDtypeStruct((cap, d), F32),
        compiler_params=_params("arbitrary"),
        name="moe_dispatch",
    )(dest, pend, h)


def _experts_kernel(be_ref, nu_ref, x_ref, w1_ref, w3_ref, w2_ref, o_ref, xb_ref, acc_ref):
    i, f = pl.program_id(0), pl.program_id(1)
    used = i < nu_ref[0]

    @pl.when(jnp.logical_and(used, f == 0))
    def _():
        xb_ref[...] = x_ref[...].astype(BF16)

    @pl.when(used)
    def _():
        x = xb_ref[...]
        h1 = _dot(x, w1_ref[0])
        act = (h1 * _sigmoid(h1) * _dot(x, w3_ref[0])).astype(BF16)
        part = _dot(act, w2_ref[0])

        @pl.when(f == 0)
        def _():
            acc_ref[...] = part

        @pl.when(f > 0)
        def _():
            acc_ref[...] += part

    @pl.when(f == pl.num_programs(1) - 1)
    def _():
        o_ref[...] = jnp.where(used, acc_ref[...], 0.0)


def _experts(xs, blk_e, n_used, w1, w3, w2, tf=MOE_TF):
    cap, d = xs.shape
    edim = w1.shape[2]
    nb = cap // MOE_BM
    return pl.pallas_call(
        _experts_kernel,
        grid_spec=pltpu.PrefetchScalarGridSpec(
            num_scalar_prefetch=2,
            grid=(nb, edim // tf),
            in_specs=[pl.BlockSpec((MOE_BM, d), lambda i, f, be, nu: (jnp.minimum(i, nu[0] - 1), 0)),
                      pl.BlockSpec((1, d, tf), lambda i, f, be, nu: (be[i], 0, f)),
                      pl.BlockSpec((1, d, tf), lambda i, f, be, nu: (be[i], 0, f)),
                      pl.BlockSpec((1, tf, d), lambda i, f, be, nu: (be[i], f, 0))],
            out_specs=pl.BlockSpec((MOE_BM, d), lambda i, f, be, nu: (i, 0)),
            scratch_shapes=[pltpu.VMEM((MOE_BM, d), BF16), pltpu.VMEM((MOE_BM, d), F32)],
        ),
        out_shape=jax.ShapeDtypeStruct((cap, d), F32),
        compiler_params=_params("arbitrary", "arbitrary"),
        name="moe_experts",
    )(blk_e, n_used, xs, w1.astype(BF16), w3.astype(BF16), w2.astype(BF16))


def _combine_kernel(dest_ref, x_ref, route_ref, yb_ref, o_ref, buf, sem):
    i = pl.program_id(0)
    nt = pl.num_programs(0)
    tm = x_ref.shape[0]
    n = nt * tm

    def row_copy(tile, slot, j, k):
        return pltpu.make_async_copy(yb_ref.at[pl.ds(dest_ref[k * n + tile * tm + j], 1)],
                                     buf.at[slot, k, pl.ds(j, 1)], sem.at[slot])

    def fetch(tile, slot):
        def body(j, c):
            row_copy(tile, slot, j, 0).start()
            row_copy(tile, slot, j, 1).start()
            return c
        lax.fori_loop(0, tm, body, 0, unroll=8)

    def drain(tile, slot):
        def body(j, c):
            row_copy(tile, slot, j, 0).wait()
            row_copy(tile, slot, j, 1).wait()
            return c
        lax.fori_loop(0, tm, body, 0, unroll=8)

    slot = i % 2

    @pl.when(i == 0)
    def _():
        fetch(0, 0)

    @pl.when(i + 1 < nt)
    def _():
        fetch(i + 1, 1 - slot)

    drain(i, slot)
    p1 = route_ref[:, 2:3]
    p2 = route_ref[:, 3:4]
    o_ref[...] = x_ref[...] + (p1 * buf[slot, 0] + p2 * buf[slot, 1])


def _combine(x, route, yb, dest, tm=512):
    n, d = x.shape
    return pl.pallas_call(
        _combine_kernel,
        grid_spec=pltpu.PrefetchScalarGridSpec(
            num_scalar_prefetch=1,
            grid=(n // tm,),
            in_specs=[pl.BlockSpec((tm, d), lambda i, dest: (i, 0)),
                      pl.BlockSpec((tm, LANES), lambda i, dest: (i, 0)),
                      pl.BlockSpec(memory_space=pl.ANY)],
            out_specs=pl.BlockSpec((tm, d), lambda i, dest: (i, 0)),
            scratch_shapes=[pltpu.VMEM((2, 2, tm, d), F32), pltpu.SemaphoreType.DMA((2,))],
        ),
        out_shape=jax.ShapeDtypeStruct((n, d), F32),
        compiler_params=_params("arbitrary"),
        name="moe_combine",
    )(dest, x, route, yb)


def _moe(x, gn, w_router, w1, w3, w2):
    n, d = x.shape
    h, route, cnt = _router(x, gn, w_router)
    counts = cnt[0, :N_EXPERTS].astype(jnp.int32)
    padded = ((counts + MOE_BM - 1) // MOE_BM) * MOE_BM
    pend = jnp.cumsum(padded).astype(jnp.int32)
    pstart = pend - padded
    experts = jnp.arange(N_EXPERTS, dtype=jnp.int32)

    def dest_of(e_col, r_col):
        e = route[:, e_col].astype(jnp.int32)
        start = jnp.sum(jnp.where(e[:, None] == experts[None, :], pstart[None, :], 0), axis=1)
        return start + route[:, r_col].astype(jnp.int32)

    dest = jnp.concatenate([dest_of(0, 4), dest_of(1, 5)])
    nb = -(-(2 * n) // MOE_BM) + N_EXPERTS
    blk_e = jnp.minimum(jnp.searchsorted(pend, jnp.arange(nb, dtype=jnp.int32) * MOE_BM, side="right"),
                        N_EXPERTS - 1).astype(jnp.int32)
    n_used = (pend[-1:] // MOE_BM).astype(jnp.int32)
    xs = _dispatch(h, dest, pend, nb * MOE_BM)
    yb = _experts(xs, blk_e, n_used, w1, w3, w2)
    return _combine(x, route, yb, dest)


def _kv_kernel(x_ref, gn_ref, wk_ref, wvt_ref, wf_ref, bf_ref, kn_ref, sel_ref, k_out, v_out, carry):
    s = pl.program_id(1)
    tm = x_ref.shape[1]

    @pl.when(s == 0)
    def _():
        carry[...] = jnp.zeros_like(carry)

    h = _rms(x_ref[0], gn_ref[...])
    hb = h.astype(BF16)
    k = _head_rms(_dot(hb, wk_ref[...]), kn_ref[...])
    v_out[0] = _dot_nt(wvt_ref[...], hb).reshape(N_HEADS, HEAD_DIM, tm).astype(BF16)
    fl = _dot(hb, wf_ref[...]) + bf_ref[...]
    log_f = jnp.minimum(fl, 0.0) - jnp.log(1.0 + jnp.exp(-jnp.abs(fl)))
    tri = (lax.broadcasted_iota(jnp.int32, (tm, tm), 0) >= lax.broadcasted_iota(jnp.int32, (tm, tm), 1)).astype(F32)
    dcum = carry[...] + _dot_hi(tri, log_f)
    carry[...] = dcum[tm - 1:tm, :]
    rest = -dcum * LOG2_E
    pieces = []
    for _ in range(N_BIAS):
        piece = rest.astype(BF16)
        pieces.append(piece)
        rest = rest - piece.astype(F32)
    pieces = jnp.concatenate(pieces, axis=1)
    lane = lax.broadcasted_iota(jnp.int32, (tm, LANES), 1)
    for hd in range(N_HEADS):
        tile = k[:, (hd // 2) * LANES:(hd // 2 + 1) * LANES]
        if hd % 2:
            tile = pltpu.roll(tile, HEAD_DIM, 1)
        bias = _dot(pieces, sel_ref[hd])
        k_out[0, hd] = jnp.where(lane < HEAD_DIM, tile, bias).astype(BF16)


def _shared_kv(x, gn, w_kvf, b_f, k_norm, tm=256):
    B, S, D = x.shape
    wk = w_kvf[:, :D].astype(BF16)
    wvt = w_kvf[:, D:2 * D].T.astype(BF16)
    wf = jnp.zeros((D, LANES), F32).at[:, :N_HEADS].set(w_kvf[:, 2 * D:]).astype(BF16)
    bf = jnp.zeros((1, LANES), F32).at[0, :N_HEADS].set(b_f)
    kn = jnp.tile(k_norm, N_HEADS).reshape(1, D)
    hh = jnp.arange(N_HEADS)[:, None, None]
    rr = jnp.arange(N_BIAS * LANES)[None, :, None]
    cc = jnp.arange(LANES)[None, None, :]
    sel = ((rr % LANES == hh) & (cc == HEAD_DIM + rr // LANES)).astype(BF16)
    tok = pl.BlockSpec((1, tm, D), lambda b, s: (b, s, 0))
    full = lambda a: pl.BlockSpec(a.shape, lambda b, s: (0,) * a.ndim)
    ins = [x, gn.reshape(1, D), wk, wvt, wf, bf, kn, sel]
    return pl.pallas_call(
        _kv_kernel,
        grid=(B, S // tm),
        in_specs=[tok] + [full(a) for a in ins[1:]],
        out_specs=[pl.BlockSpec((1, N_HEADS, tm, LANES), lambda b, s: (b, 0, s, 0)),
                   pl.BlockSpec((1, N_HEADS, HEAD_DIM, tm), lambda b, s: (b, 0, 0, s))],
        out_shape=[jax.ShapeDtypeStruct((B, N_HEADS, S, LANES), BF16),
                   jax.ShapeDtypeStruct((B, N_HEADS, HEAD_DIM, S), BF16)],
        scratch_shapes=[pltpu.VMEM((1, LANES), F32)],
        compiler_params=_params("arbitrary", "arbitrary"),
        name="shared_kv",
    )(*ins)


def _qg_kernel(x_ref, gn_ref, wqt_ref, wgt_ref, qn_ref, q_out, og_out):
    tm = x_ref.shape[1]
    hb = _rms(x_ref[0], gn_ref[...]).astype(BF16)
    q = _dot_nt(wqt_ref[...], hb).reshape(N_HEADS, HEAD_DIM, tm)
    q = q * lax.rsqrt(jnp.mean(q * q, axis=1, keepdims=True) + RMS_EPS) * qn_ref[...]
    q_out[0, :, :HEAD_DIM, :] = (q * (HEAD_DIM ** -0.5 * LOG2_E)).astype(BF16)
    row = lax.broadcasted_iota(jnp.int32, (N_HEADS, LANES - HEAD_DIM, tm), 1)
    q_out[0, :, HEAD_DIM:, :] = (row < N_BIAS).astype(BF16)
    og_out[0] = _dot_nt(wgt_ref[...], hb).reshape(N_HEADS, HEAD_DIM, tm).astype(BF16)


def _qg_proj(x, gn, w_qg, q_norm, tm=512):
    B, S, D = x.shape
    ins = [x, gn.reshape(1, D), w_qg[:, :D].T.astype(BF16), w_qg[:, D:].T.astype(BF16),
           q_norm.reshape(1, HEAD_DIM, 1)]
    const = lambda a: pl.BlockSpec(a.shape, lambda b, s: (0,) * a.ndim)
    return pl.pallas_call(
        _qg_kernel,
        grid=(B, S // tm),
        in_specs=[pl.BlockSpec((1, tm, D), lambda b, s: (b, s, 0))] + [const(a) for a in ins[1:]],
        out_specs=[pl.BlockSpec((1, N_HEADS, LANES, tm), lambda b, s: (b, 0, 0, s)),
                   pl.BlockSpec((1, N_HEADS, HEAD_DIM, tm), lambda b, s: (b, 0, 0, s))],
        out_shape=[jax.ShapeDtypeStruct((B, N_HEADS, LANES, S), BF16),
                   jax.ShapeDtypeStruct((B, N_HEADS, HEAD_DIM, S), BF16)],
        compiler_params=_params("arbitrary", "arbitrary"),
        name="fox_qg",
    )(*ins)


def _proj_res_t_kernel(a_ref, gate_ref, w_ref, res_ref, o_ref):
    tm = res_ref.shape[1]
    a = (a_ref[0].astype(F32) * _sigmoid(gate_ref[0].astype(F32))).astype(BF16)
    o_ref[0] = res_ref[0] + _dot_tn(a.reshape(D_MODEL, tm), w_ref[...])


def _proj_res_t(a_t, gate_t, w, res, tm=512):
    B, S, D = res.shape
    head_t = pl.BlockSpec((1, N_HEADS, HEAD_DIM, tm), lambda b, s: (b, 0, 0, s))
    tok = pl.BlockSpec((1, tm, D), lambda b, s: (b, s, 0))
    return pl.pallas_call(
        _proj_res_t_kernel,
        grid=(B, S // tm),
        in_specs=[head_t, head_t, pl.BlockSpec((D, D), lambda b, s: (0, 0)), tok],
        out_specs=tok,
        out_shape=jax.ShapeDtypeStruct((B, S, D), F32),
        compiler_params=_params("arbitrary", "arbitrary"),
        name="proj_res_gated",
    )(a_t, gate_t, w.astype(BF16), res)


def _attn_kernel(qt_ref, k_ref, vt_ref, o_ref):
    tq, tk = ATT_TQ, ATT_TK
    i = pl.program_id(2)
    hs = range(ATT_HEADS)
    qt = [qt_ref[0, h] for h in hs]
    q_pos = i * tq + lax.broadcasted_iota(jnp.int32, (tk, tq), 1)
    k_off = lax.broadcasted_iota(jnp.int32, (tk, tq), 0)

    def step(j, carry, masked):
        off = pl.multiple_of(j * tk, tk)
        s = [_dot(k_ref[0, h, pl.ds(off, tk), :], qt[h]) for h in hs]
        if masked:
            keep = k_off + off <= q_pos
            s = [jnp.where(keep, x, -jnp.inf) for x in s]
        m_new = [jnp.maximum(carry[h][0], jnp.max(s[h], axis=0, keepdims=True)) for h in hs]
        alpha = [jnp.exp2(carry[h][0] - m_new[h]) for h in hs]
        p = [jnp.exp2(s[h] - m_new[h]) for h in hs]
        l = [alpha[h] * carry[h][1] + jnp.sum(p[h], axis=0, keepdims=True) for h in hs]
        pv = [_dot(vt_ref[0, h, :, pl.ds(off, tk)], p[h].astype(BF16)) for h in hs]
        return tuple((m_new[h], l[h], alpha[h] * carry[h][2] + pv[h]) for h in hs)

    init = tuple((jnp.full((1, tq), -jnp.inf, F32), jnp.zeros((1, tq), F32), jnp.zeros((HEAD_DIM, tq), F32))
                 for _ in hs)
    carry = lax.fori_loop(0, i, lambda j, c: step(j, c, False), init)
    carry = step(i, carry, True)
    for h in hs:
        _, l, acc = carry[h]
        o_ref[0, h] = (acc / l).astype(BF16)


def _fox_attn(qt, k_aug, vt):
    B, H, _, S = qt.shape
    g = ATT_HEADS
    return pl.pallas_call(
        _attn_kernel,
        grid=(B, H // g, S // ATT_TQ),
        in_specs=[pl.BlockSpec((1, g, LANES, ATT_TQ), lambda b, h, i: (b, h, 0, i)),
                  pl.BlockSpec((1, g, S, LANES), lambda b, h, i: (b, h, 0, 0)),
                  pl.BlockSpec((1, g, HEAD_DIM, S), lambda b, h, i: (b, h, 0, 0))],
        out_specs=pl.BlockSpec((1, g, HEAD_DIM, ATT_TQ), lambda b, h, i: (b, h, 0, i)),
        out_shape=jax.ShapeDtypeStruct((B, H, HEAD_DIM, S), BF16),
        compiler_params=_params("arbitrary", "arbitrary", "arbitrary"),
        name="fox_attn",
    )(qt, k_aug, vt)


def kernel(x, norm_mix, norm_ffn, rk_mu, rk_w_r, rk_w_k, rk_w_v, rk_w_o, rk_w0, rk_w1, rk_w2, rk_a0, rk_a1, rk_a2, rk_v0, rk_v1, rk_v2, rk_g1, rk_g2, rk_k_k, rk_k_a, rk_r_k, rk_lnx_w, rk_lnx_b, kv_norm, w_kvf, b_f, k_norm, fx_w_qg, fx_q_norm, fx_w_o, ffn_w1, ffn_w3, ffn_w2, moe_router, moe_w1, moe_w3, moe_w2):
    B, S, D = x.shape
    n = B * S
    depth = norm_mix.shape[0]
    n_rwkv = rk_mu.shape[0]
    v_first = None
    kv = None
    for l in range(depth):
        if l < n_rwkv:
            vres = None if l == 0 else (rk_v0[l - 1], rk_v1[l - 1], rk_v2[l - 1])
            r, lw, k, v, a, g = _rwkv_proj(
                x, norm_mix[l], rk_mu[l], rk_w_r[l], rk_w_k[l], rk_w_v[l], rk_w0[l], rk_w1[l], rk_w2[l],
                rk_a0[l], rk_a1[l], rk_a2[l], rk_g1[l], rk_g2[l], vres, v_first)
            if l == 0:
                v_first = v
            z = _wkv(r, lw, k, v, a, g, rk_k_k[l], rk_k_a[l], rk_r_k[l], rk_lnx_w[l], rk_lnx_b[l])
            x2 = _proj_res(z.reshape(n, D), rk_w_o[l], x.reshape(n, D))
        else:
            if kv is None:
                kv = _shared_kv(x, kv_norm, w_kvf, b_f, k_norm)
            j = l - n_rwkv
            q_t, og_t = _qg_proj(x, norm_mix[l], fx_w_qg[j], fx_q_norm[j])
            o_t = _fox_attn(q_t, *kv)
            x2 = _proj_res_t(o_t, og_t, fx_w_o[j], x).reshape(n, D)
        i = l // 2
        if l % 2 == 0:
            x2 = _ffn(x2, norm_ffn[l], ffn_w1[i], ffn_w3[i], ffn_w2[i])
        else:
            x2 = _moe(x2, norm_ffn[l], moe_router[i], moe_w1[i], moe_w3[i], moe_w2[i])
        x = x2.reshape(B, S, D)
    return x
```

```python
import functools

import jax
import jax.numpy as jnp
from jax import lax
from jax.experimental import pallas as pl
from jax.experimental.pallas import tpu as pltpu

F32 = jnp.float32
BF16 = jnp.bfloat16
HI = lax.Precision.HIGHEST

D_MODEL = 1024
HEAD_DIM = 64
N_HEADS = D_MODEL // HEAD_DIM
LANES = 128
HEAD_PAIRS = D_MODEL // LANES
N_EXPERTS = 8
LOG2_E = 1.4426950408889634
N_BIAS = 3
V_ROWS = HEAD_DIM + 16
RMS_EPS = 1e-6
GN_EPS = 64e-5
VMEM_LIMIT_BYTES = 56 * 1024 * 1024

WKV_CHUNK = 64
WKV_TILE = 512
WKV_PAIRS = 2
ATT_TQ = 512
ATT_TK = 512
ATT_HEADS = 4
MOE_BM = 1024
MOE_TF = 896


def _params(*sem):
    return pltpu.CompilerParams(dimension_semantics=sem, vmem_limit_bytes=VMEM_LIMIT_BYTES)


def _dot(a, b):
    return jnp.dot(a, b, preferred_element_type=F32)


def _dot_hi(a, b):
    return jnp.dot(a, b, preferred_element_type=F32, precision=HI)


def _dot_nt(a, b):
    return lax.dot_general(a, b, (((1,), (1,)), ((), ())), preferred_element_type=F32)


def _dot_tn(a, b):
    return lax.dot_general(a, b, (((0,), (0,)), ((), ())), preferred_element_type=F32)


def _rms(x, g):
    return x * lax.rsqrt(jnp.mean(x * x, axis=-1, keepdims=True) + RMS_EPS) * g


def _sigmoid(x):
    return 1.0 / (1.0 + jnp.exp(-x))


def _pair_ones():
    ri = lax.broadcasted_iota(jnp.int32, (LANES, LANES), 0)
    ci = lax.broadcasted_iota(jnp.int32, (LANES, LANES), 1)
    return ((ri < HEAD_DIM) == (ci < HEAD_DIM)).astype(F32)


def _head_rms(x, gain_row):
    ones = _pair_ones()
    parts = []
    for p in range(HEAD_PAIRS):
        xs = x[:, p * LANES:(p + 1) * LANES]
        ms = _dot_hi(xs * xs, ones) * (1.0 / HEAD_DIM)
        parts.append(xs * lax.rsqrt(ms + RMS_EPS))
    return jnp.concatenate(parts, axis=1) * gain_row


def _rwkv_proj_kernel(has_vres, *refs):
    if has_vres:
        (x_ref, gn_ref, mu_ref, wr_ref, wk_ref, wv_ref, w0_ref, w1_ref, w2_ref, a0_ref, a1_ref,
         a2_ref, g1_ref, g2_ref, v0_ref, v1_ref, v2_ref, vf_ref,
         r_out, lw_out, k_out, v_out, a_out, g_out, carry) = refs
    else:
        (x_ref, gn_ref, mu_ref, wr_ref, wk_ref, wv_ref, w0_ref, w1_ref, w2_ref, a0_ref, a1_ref,
         a2_ref, g1_ref, g2_ref,
         r_out, lw_out, k_out, v_out, a_out, g_out, carry) = refs
    s = pl.program_id(1)
    tm = x_ref.shape[1]

    @pl.when(s == 0)
    def _():
        carry[...] = jnp.zeros_like(carry)

    h = _rms(x_ref[0], gn_ref[...])
    prev_last = carry[...]
    carry[...] = h[tm - 1:tm, :]
    row = lax.broadcasted_iota(jnp.int32, (tm, 1), 0)
    xx = jnp.where(row == 0, prev_last, pltpu.roll(h, 1, 0)) - h

    def mix(i):
        return (h + xx * mu_ref[i:i + 1, :]).astype(BF16)

    xr, xw, xk, xv, xa, xg = [mix(i) for i in range(6)]
    r_out[0] = _dot(xr, wr_ref[...]).astype(BF16)
    k_out[0] = _dot(xk, wk_ref[...]).astype(BF16)
    v = _dot(xv, wv_ref[...])
    wl = w0_ref[...] + _dot(jnp.tanh(_dot(xw, w1_ref[...])).astype(BF16), w2_ref[...])
    w = -(jnp.maximum(-wl, 0.0) + jnp.log(1.0 + jnp.exp(-jnp.abs(wl)))) - 0.5
    lw_out[0] = -jnp.exp(w)
    a_out[0] = _sigmoid(a0_ref[...] + _dot(_dot(xa, a1_ref[...]).astype(BF16), a2_ref[...])).astype(BF16)
    g_out[0] = _dot(_sigmoid(_dot(xg, g1_ref[...])).astype(BF16), g2_ref[...]).astype(BF16)
    if has_vres:
        gate = _sigmoid(v0_ref[...] + _dot(_dot(xv, v1_ref[...]).astype(BF16), v2_ref[...]))
        v = v + (vf_ref[0].astype(F32) - v) * gate
    v_out[0] = v.astype(BF16)


def _rwkv_proj(x, gn, mu, wr, wk, wv, w0, w1, w2, a0, a1, a2, g1, g2, vres, v_first, tm=512):
    B, S, D = x.shape
    has_vres = vres is not None
    row = lambda a: a.reshape(1, -1)
    full = lambda a: pl.BlockSpec(a.shape, lambda b, s: (0,) * a.ndim, pipeline_mode=pl.Buffered(1))
    tok = pl.BlockSpec((1, tm, D), lambda b, s: (b, s, 0))
    ins = [x, row(gn), mu, wr.astype(BF16), wk.astype(BF16), wv.astype(BF16), row(w0),
           w1.astype(BF16), w2.astype(BF16), row(a0), a1.astype(BF16), a2.astype(BF16),
           g1.astype(BF16), g2.astype(BF16)]
    if has_vres:
        v0, v1, v2 = vres
        ins += [row(v0), v1.astype(BF16), v2.astype(BF16)]
    specs = [tok] + [full(a) for a in ins[1:]]
    if has_vres:
        ins.append(v_first)
        specs.append(tok)
    out_shape = [jax.ShapeDtypeStruct((B, S, D), dt) for dt in (BF16, F32, BF16, BF16, BF16, BF16)]
    return pl.pallas_call(
        functools.partial(_rwkv_proj_kernel, has_vres),
        grid=(B, S // tm),
        in_specs=specs,
        out_specs=[tok] * 6,
        out_shape=out_shape,
        scratch_shapes=[pltpu.VMEM((1, D), F32)],
        compiler_params=_params("arbitrary", "arbitrary"),
        name="rwkv_proj_vres" if has_vres else "rwkv_proj",
    )(*ins)


def _head_sum(x, ones_b):
    hi = x.astype(BF16)
    lo = (x - hi.astype(F32)).astype(BF16)
    return _dot(hi, ones_b) + _dot(lo, ones_b)


def _wkv_kernel(r_ref, lw_ref, k_ref, v_ref, a_ref, g_ref, kk_ref, ka_ref, rk_ref, lnw_ref, lnb_ref,
                z_ref, state):
    C = WKV_CHUNK
    P = 2 * C
    tt = r_ref.shape[1]
    nc = tt // C

    @pl.when(pl.program_id(2) == 0)
    def _():
        state[...] = jnp.zeros_like(state)

    lane = lax.broadcasted_iota(jnp.int32, (C, LANES), 1)
    m0 = (lane < HEAD_DIM).astype(F32)
    m1 = 1.0 - m0
    ri = lax.broadcasted_iota(jnp.int32, (P, P), 0)
    ci = lax.broadcasted_iota(jnp.int32, (P, P), 1)
    strict = ri > ci
    incl = ri >= ci
    eye = (ri == ci).astype(F32)
    tri = (lax.broadcasted_iota(jnp.int32, (C, C), 0) >= lax.broadcasted_iota(jnp.int32, (C, C), 1)).astype(F32)
    ones_b = _pair_ones().astype(BF16)

    def stack(x):
        return jnp.concatenate([x * m0, x * m1], axis=0).astype(BF16)

    nq = r_ref.shape[2] // LANES
    cs = range(nc)
    pre = []
    for q in range(nq):
        ls = slice(q * LANES, (q + 1) * LANES)
        r = r_ref[0, :, ls].astype(F32)
        k = k_ref[0, :, ls].astype(F32)
        v = v_ref[0, :, ls].astype(F32)
        a = a_ref[0, :, ls].astype(F32)
        lw = lw_ref[0, :, ls]
        kk = k * kk_ref[:, ls]
        kk = kk * lax.rsqrt(jnp.maximum(_head_sum(kk * kk, ones_b), 1e-24))
        k2 = k * (1.0 + (a - 1.0) * ka_ref[:, ls])
        bv = kk * a
        cum_c = _dot_hi(tri, jnp.concatenate([lw[c * C:(c + 1) * C] for c in cs], axis=1))
        cum = jnp.concatenate([cum_c[:, c * LANES:(c + 1) * LANES] for c in cs], axis=0)
        cum_end = [cum_c[C - 1:C, c * LANES:(c + 1) * LANES] for c in cs]
        to_end = jnp.concatenate([jnp.broadcast_to(ce, (C, LANES)) for ce in cum_end], axis=0) - cum
        g_inv = jnp.exp(-cum)
        g_end = jnp.exp(to_end)
        pre.append(dict(r=r, k2=k2, v=v, cum_end=cum_end, a_t=-kk * jnp.exp(cum - lw), r_t=r * jnp.exp(cum),
                        b_t=bv * g_inv, k_t=k2 * g_inv, b_h=bv * g_end, k_h=k2 * g_end))

    units = [(q, c) for c in cs for q in range(nq)]
    us = range(len(units))

    def stacks(name):
        return [stack(pre[q][name][c * C:(c + 1) * C]) for q, c in units]

    a_s, r_s, b_s, k_s, v_s, bh_s, kh_s = (stacks(x) for x in ("a_t", "r_t", "b_t", "k_t", "v", "b_h", "k_h"))
    cat0 = lambda x, y: jnp.concatenate([x, y], axis=0)
    cat1 = lambda x, y: jnp.concatenate([x, y], axis=1)
    gram = [_dot_nt(cat0(a_s[u], r_s[u]), cat0(b_s[u], k_s[u])) for u in us]
    n_ab = [jnp.where(strict, g[:P, :P], 0.0) for g in gram]
    a_ak = [jnp.where(strict, g[:P, P:], 0.0).astype(BF16) for g in gram]
    a_rb = [jnp.where(incl, g[P:, :P], 0.0).astype(BF16) for g in gram]
    a_rk = [jnp.where(incl, g[P:, P:], 0.0).astype(BF16) for g in gram]
    akv = [_dot(a_ak[u], v_s[u]).astype(BF16) for u in us]
    pw = [n.astype(BF16) for n in n_ab]
    inv = [eye + n for n in n_ab]
    pw = [_dot(p, p).astype(BF16) for p in pw]
    for _ in range(4):
        both = [_dot(cat0(pw[u], inv[u].astype(BF16)), pw[u]) for u in us]
        pw = [x[:P].astype(BF16) for x in both]
        inv = [inv[u] + both[u][P:] for u in us]
    inv = [(inv[u] + _dot(inv[u].astype(BF16), pw[u])).astype(BF16) for u in us]
    wu = [_dot(inv[u], cat1(a_s[u], akv[u])).astype(BF16) for u in us]
    rb = [_dot(a_rb[u], wu[u]) for u in us]
    q_h = [(r_s[u].astype(F32) + rb[u][:, :LANES]).astype(BF16) for u in us]
    y0 = [rb[u][:, LANES:] + _dot(a_rk[u], v_s[u]) for u in us]
    m_st = [_dot_tn(wu[u][:, :LANES], bh_s[u]).astype(BF16) for u in us]
    z_st = [_dot_tn(cat0(wu[u][:, LANES:], v_s[u]), cat0(bh_s[u], kh_s[u])) for u in us]

    s = [state[q] for q in range(nq)]
    ys = [[] for _ in range(nq)]
    for c in cs:
        for q in range(nq):
            u = c * nq + q
            sb = s[q].astype(BF16)
            y_s = _dot_nt(q_h[u], sb) + y0[u]
            ys[q].append(y_s[:C] + y_s[C:])
            s[q] = s[q] * jnp.exp(pre[q]["cum_end"][c]) + _dot(sb, m_st[u]) + z_st[u]
    for q in range(nq):
        state[q] = s[q]

    for q in range(nq):
        ls = slice(q * LANES, (q + 1) * LANES)
        y = jnp.concatenate(ys[q], axis=0)
        mean = _head_sum(y, ones_b) * (1.0 / HEAD_DIM)
        yc = y - mean
        var = _head_sum(yc * yc, ones_b) * (1.0 / HEAD_DIM)
        yn = yc * lax.rsqrt(var + GN_EPS) * lnw_ref[:, ls] + lnb_ref[:, ls]
        bonus = _head_sum(pre[q]["r"] * pre[q]["k2"] * rk_ref[:, ls], ones_b) * pre[q]["v"]
        z_ref[0, :, ls] = ((yn + bonus) * g_ref[0, :, ls].astype(F32)).astype(BF16)


def _wkv(r, lw, k, v, a, g, k_k, k_a, r_k, ln_w, ln_b):
    B, S, D = r.shape
    tt = min(WKV_TILE, S)
    width = WKV_PAIRS * LANES
    tok = pl.BlockSpec((1, tt, width), lambda b, p, t: (b, t, p))
    par = pl.BlockSpec((1, width), lambda b, p, t: (0, p))
    row = lambda x: x.reshape(1, D)
    return pl.pallas_call(
        _wkv_kernel,
        grid=(B, D // width, S // tt),
        in_specs=[tok] * 6 + [par] * 5,
        out_specs=tok,
        out_shape=jax.ShapeDtypeStruct((B, S, D), BF16),
        scratch_shapes=[pltpu.VMEM((WKV_PAIRS, LANES, LANES), F32)],
        compiler_params=_params("arbitrary", "arbitrary", "arbitrary"),
        name="wkv7",
    )(r, lw, k, v, a, g, row(k_k), row(k_a), row(r_k), row(ln_w), row(ln_b))


def _proj_res_kernel(a_ref, w_ref, res_ref, o_ref):
    o_ref[...] = res_ref[...] + _dot(a_ref[...], w_ref[...])


def _proj_res(a, w, res, tm=512):
    n, kdim = a.shape
    d = w.shape[1]
    return pl.pallas_call(
        _proj_res_kernel,
        grid=(n // tm,),
        in_specs=[pl.BlockSpec((tm, kdim), lambda i: (i, 0)), pl.BlockSpec((kdim, d), lambda i: (0, 0)),
                  pl.BlockSpec((tm, d), lambda i: (i, 0))],
        out_specs=pl.BlockSpec((tm, d), lambda i: (i, 0)),
        out_shape=jax.ShapeDtypeStruct((n, d), F32),
        compiler_params=_params("arbitrary"),
        name="proj_res",
    )(a, w.astype(BF16), res)


def _ffn_kernel(x_ref, gn_ref, w1_ref, w3_ref, w2_ref, o_ref, h_ref, acc_ref):
    f = pl.program_id(1)

    @pl.when(f == 0)
    def _():
        h_ref[...] = _rms(x_ref[...], gn_ref[...]).astype(BF16)

    h = h_ref[...]
    h1 = _dot(h, w1_ref[...])
    act = (h1 * _sigmoid(h1) * _dot(h, w3_ref[...])).astype(BF16)
    part = _dot(act, w2_ref[...])

    @pl.when(f == 0)
    def _():
        acc_ref[...] = part

    @pl.when(f > 0)
    def _():
        acc_ref[...] += part

    @pl.when(f == pl.num_programs(1) - 1)
    def _():
        o_ref[...] = x_ref[...] + acc_ref[...]


def _ffn(x, gn, w1, w3, w2, tm=512, tf=1408):
    n, d = x.shape
    fdim = w1.shape[1]
    return pl.pallas_call(
        _ffn_kernel,
        grid=(n // tm, fdim // tf),
        in_specs=[pl.BlockSpec((tm, d), lambda i, f: (i, 0)),
                  pl.BlockSpec((1, d), lambda i, f: (0, 0)),
                  pl.BlockSpec((d, tf), lambda i, f: (0, f)),
                  pl.BlockSpec((d, tf), lambda i, f: (0, f)),
                  pl.BlockSpec((tf, d), lambda i, f: (f, 0))],
        out_specs=pl.BlockSpec((tm, d), lambda i, f: (i, 0)),
        out_shape=jax.ShapeDtypeStruct((n, d), F32),
        scratch_shapes=[pltpu.VMEM((tm, d), BF16), pltpu.VMEM((tm, d), F32)],
        compiler_params=_params("arbitrary", "arbitrary"),
        name="ffn_dense",
    )(x, gn.reshape(1, d), w1.astype(BF16), w3.astype(BF16), w2.astype(BF16))


def _router_kernel(x_ref, gn_ref, wr_ref, h_ref, route_ref, cnt_ref, carry):
    tm = x_ref.shape[0]

    @pl.when(pl.program_id(0) == 0)
    def _():
        carry[...] = jnp.zeros_like(carry)

    h = _rms(x_ref[...], gn_ref[...])
    h_ref[...] = h
    logits = _dot_hi(h, wr_ref[...])
    lane = lax.broadcasted_iota(jnp.int32, logits.shape, 1)
    neg = jnp.float32(-jnp.inf)
    logits = jnp.where(lane < N_EXPERTS, logits, neg)
    m1 = jnp.max(logits, axis=-1, keepdims=True)
    i1 = jnp.min(jnp.where(logits == m1, lane, LANES), axis=-1, keepdims=True)
    rest = jnp.where(lane == i1, neg, logits)
    m2 = jnp.max(rest, axis=-1, keepdims=True)
    i2 = jnp.min(jnp.where(rest == m2, lane, LANES), axis=-1, keepdims=True)
    e2 = jnp.exp(m2 - m1)
    p1 = 1.0 / (1.0 + e2)
    p2 = e2 * p1
    oh1 = (lane == i1).astype(F32)
    oh2 = (lane == i2).astype(F32)
    cnt = oh1 + oh2
    earlier = (lax.broadcasted_iota(jnp.int32, (tm, tm), 0) > lax.broadcasted_iota(jnp.int32, (tm, tm), 1))
    before = carry[...] + _dot(earlier.astype(BF16), cnt.astype(BF16))
    r1 = jnp.sum(oh1 * before, axis=-1, keepdims=True)
    r2 = jnp.sum(oh2 * before, axis=-1, keepdims=True)
    total = carry[...] + jnp.sum(cnt, axis=0, keepdims=True)
    carry[...] = total
    cnt_ref[...] = total
    cols = (i1.astype(F32), i2.astype(F32), p1, p2, r1, r2)
    route = jnp.zeros(logits.shape, F32)
    for c, val in enumerate(cols):
        route = jnp.where(lane == c, val, route)
    route_ref[...] = route


def _router(x, gn, w_router, tm=512):
    n, d = x.shape
    wr = jnp.zeros((d, LANES), F32).at[:, :N_EXPERTS].set(w_router)
    return pl.pallas_call(
        _router_kernel,
        grid=(n // tm,),
        in_specs=[pl.BlockSpec((tm, d), lambda i: (i, 0)),
                  pl.BlockSpec((1, d), lambda i: (0, 0)),
                  pl.BlockSpec((d, LANES), lambda i: (0, 0))],
        out_specs=[pl.BlockSpec((tm, d), lambda i: (i, 0)), pl.BlockSpec((tm, LANES), lambda i: (i, 0)),
                   pl.BlockSpec((1, LANES), lambda i: (0, 0))],
        out_shape=[jax.ShapeDtypeStruct((n, d), F32), jax.ShapeDtypeStruct((n, LANES), F32),
                   jax.ShapeDtypeStruct((1, LANES), F32)],
        scratch_shapes=[pltpu.VMEM((1, LANES), F32)],
        compiler_params=_params("arbitrary"),
        name="moe_router",
    )(x, gn.reshape(1, d), wr)


def _dispatch_kernel(dest_ref, pend_ref, h_ref, xs_ref, zeros, sem, zsem):
    i = pl.program_id(0)
    tm = h_ref.shape[0]
    n = pl.num_programs(0) * tm

    @pl.when(i == 0)
    def _():
        zeros[...] = jnp.zeros_like(zeros)

        def tail(e):
            start = pl.multiple_of(pend_ref[e] - MOE_BM, MOE_BM)
            return pltpu.make_async_copy(zeros, xs_ref.at[pl.ds(start, MOE_BM)], zsem)

        def nonempty(e):
            return pend_ref[e] > (pend_ref[e - 1] if e else 0)

        for e in range(N_EXPERTS):
            pl.when(nonempty(e))(lambda e=e: tail(e).start())
        for e in range(N_EXPERTS):
            pl.when(nonempty(e))(lambda e=e: tail(e).wait())

        def spare(b):
            return pltpu.make_async_copy(zeros, xs_ref.at[pl.ds(pl.multiple_of(b * MOE_BM, MOE_BM), MOE_BM)], zsem)

        first_spare = pend_ref[N_EXPERTS - 1] // MOE_BM
        n_blocks = xs_ref.shape[0] // MOE_BM
        lax.fori_loop(first_spare, n_blocks, lambda b, c: (spare(b).start(), c)[1], 0)
        lax.fori_loop(first_spare, n_blocks, lambda b, c: (spare(b).wait(), c)[1], 0)

    def row_copy(j, k):
        return pltpu.make_async_copy(h_ref.at[pl.ds(j, 1)],
                                     xs_ref.at[pl.ds(dest_ref[k * n + i * tm + j], 1)], sem)

    def start(j, c):
        row_copy(j, 0).start()
        row_copy(j, 1).start()
        return c

    def wait(j, c):
        row_copy(j, 0).wait()
        row_copy(j, 1).wait()
        return c

    lax.fori_loop(0, tm, start, 0, unroll=8)
    lax.fori_loop(0, tm, wait, 0, unroll=8)


def _dispatch(h, dest, pend, cap, tm=512):
    n, d = h.shape
    return pl.pallas_call(
        _dispatch_kernel,
        grid_spec=pltpu.PrefetchScalarGridSpec(
            num_scalar_prefetch=2,
            grid=(n // tm,),
            in_specs=[pl.BlockSpec((tm, d), lambda i, dest, pend: (i, 0))],
            out_specs=pl.BlockSpec(memory_space=pl.ANY),
            scratch_shapes=[pltpu.VMEM((MOE_BM, d), F32), pltpu.SemaphoreType.DMA(()),
                            pltpu.SemaphoreType.DMA(())],
        ),
        out_shape=jax.ShapeDtypeStruct((cap, d), F32),
        compiler_params=_params("arbitrary"),
        name="moe_dispatch",
    )(dest, pend, h)


def _experts_kernel(be_ref, nu_ref, x_ref, w1_ref, w3_ref, w2_ref, o_ref, xb_ref, acc_ref):
    i, f = pl.program_id(0), pl.program_id(1)
    used = i < nu_ref[0]

    @pl.when(jnp.logical_and(used, f == 0))
    def _():
        xb_ref[...] = x_ref[...].astype(BF16)

    @pl.when(used)
    def _():
        x = xb_ref[...]
        h1 = _dot(x, w1_ref[0])
        act = (h1 * _sigmoid(h1) * _dot(x, w3_ref[0])).astype(BF16)
        part = _dot(act, w2_ref[0])

        @pl.when(f == 0)
        def _():
            acc_ref[...] = part

        @pl.when(f > 0)
        def _():
            acc_ref[...] += part

    @pl.when(f == pl.num_programs(1) - 1)
    def _():
        o_ref[...] = jnp.where(used, acc_ref[...], 0.0)


def _experts(xs, blk_e, n_used, w1, w3, w2, tf=MOE_TF):
    cap, d = xs.shape
    edim = w1.shape[2]
    nb = cap // MOE_BM
    return pl.pallas_call(
        _experts_kernel,
        grid_spec=pltpu.PrefetchScalarGridSpec(
            num_scalar_prefetch=2,
            grid=(nb, edim // tf),
            in_specs=[pl.BlockSpec((MOE_BM, d), lambda i, f, be, nu: (jnp.minimum(i, nu[0] - 1), 0)),
                      pl.BlockSpec((1, d, tf), lambda i, f, be, nu: (be[i], 0, f)),
                      pl.BlockSpec((1, d, tf), lambda i, f, be, nu: (be[i], 0, f)),
                      pl.BlockSpec((1, tf, d), lambda i, f, be, nu: (be[i], f, 0))],
            out_specs=pl.BlockSpec((MOE_BM, d), lambda i, f, be, nu: (i, 0)),
            scratch_shapes=[pltpu.VMEM((MOE_BM, d), BF16), pltpu.VMEM((MOE_BM, d), F32)],
        ),
        out_shape=jax.ShapeDtypeStruct((cap, d), F32),
        compiler_params=_params("arbitrary", "arbitrary"),
        name="moe_experts",
    )(blk_e, n_used, xs, w1.astype(BF16), w3.astype(BF16), w2.astype(BF16))


def _combine_kernel(dest_ref, x_ref, route_ref, yb_ref, o_ref, buf, sem):
    i = pl.program_id(0)
    nt = pl.num_programs(0)
    tm = x_ref.shape[0]
    n = nt * tm

    def row_copy(tile, slot, j, k):
        return pltpu.make_async_copy(yb_ref.at[pl.ds(dest_ref[k * n + tile * tm + j], 1)],
                                     buf.at[slot, k, pl.ds(j, 1)], sem.at[slot])

    def fetch(tile, slot):
        def body(j, c):
            row_copy(tile, slot, j, 0).start()
            row_copy(tile, slot, j, 1).start()
            return c
        lax.fori_loop(0, tm, body, 0, unroll=8)

    def drain(tile, slot):
        def body(j, c):
            row_copy(tile, slot, j, 0).wait()
            row_copy(tile, slot, j, 1).wait()
            return c
        lax.fori_loop(0, tm, body, 0, unroll=8)

    slot = i % 2

    @pl.when(i == 0)
    def _():
        fetch(0, 0)

    @pl.when(i + 1 < nt)
    def _():
        fetch(i + 1, 1 - slot)

    drain(i, slot)
    p1 = route_ref[:, 2:3]
    p2 = route_ref[:, 3:4]
    o_ref[...] = x_ref[...] + (p1 * buf[slot, 0] + p2 * buf[slot, 1])


def _combine(x, route, yb, dest, tm=512):
    n, d = x.shape
    return pl.pallas_call(
        _combine_kernel,
        grid_spec=pltpu.PrefetchScalarGridSpec(
            num_scalar_prefetch=1,
            grid=(n // tm,),
            in_specs=[pl.BlockSpec((tm, d), lambda i, dest: (i, 0)),
                      pl.BlockSpec((tm, LANES), lambda i, dest: (i, 0)),
                      pl.BlockSpec(memory_space=pl.ANY)],
            out_specs=pl.BlockSpec((tm, d), lambda i, dest: (i, 0)),
            scratch_shapes=[pltpu.VMEM((2, 2, tm, d), F32), pltpu.SemaphoreType.DMA((2,))],
        ),
        out_shape=jax.ShapeDtypeStruct((n, d), F32),
        compiler_params=_params("arbitrary"),
        name="moe_combine",
    )(dest, x, route, yb)


def _moe(x, gn, w_router, w1, w3, w2):
    n, d = x.shape
    h, route, cnt = _router(x, gn, w_router)
    counts = cnt[0, :N_EXPERTS].astype(jnp.int32)
    padded = ((counts + MOE_BM - 1) // MOE_BM) * MOE_BM
    pend = jnp.cumsum(padded).astype(jnp.int32)
    pstart = pend - padded
    experts = jnp.arange(N_EXPERTS, dtype=jnp.int32)

    def dest_of(e_col, r_col):
        e = route[:, e_col].astype(jnp.int32)
        start = jnp.sum(jnp.where(e[:, None] == experts[None, :], pstart[None, :], 0), axis=1)
        return start + route[:, r_col].astype(jnp.int32)

    dest = jnp.concatenate([dest_of(0, 4), dest_of(1, 5)])
    nb = -(-(2 * n) // MOE_BM) + N_EXPERTS
    blk_e = jnp.minimum(jnp.searchsorted(pend, jnp.arange(nb, dtype=jnp.int32) * MOE_BM, side="right"),
                        N_EXPERTS - 1).astype(jnp.int32)
    n_used = (pend[-1:] // MOE_BM).astype(jnp.int32)
    xs = _dispatch(h, dest, pend, nb * MOE_BM)
    yb = _experts(xs, blk_e, n_used, w1, w3, w2)
    return _combine(x, route, yb, dest)


def _kv_kernel(x_ref, gn_ref, wk_ref, wvt_ref, wf_ref, bf_ref, kn_ref, sel_ref, k_out, v_out, carry):
    s = pl.program_id(1)
    tm = x_ref.shape[1]

    @pl.when(s == 0)
    def _():
        carry[...] = jnp.zeros_like(carry)

    h = _rms(x_ref[0], gn_ref[...])
    hb = h.astype(BF16)
    k = _head_rms(_dot(hb, wk_ref[...]), kn_ref[...])
    v_out[0, :, :HEAD_DIM, :] = _dot_nt(wvt_ref[...], hb).reshape(N_HEADS, HEAD_DIM, tm).astype(BF16)
    row = lax.broadcasted_iota(jnp.int32, (N_HEADS, V_ROWS - HEAD_DIM, tm), 1)
    v_out[0, :, HEAD_DIM:, :] = (row == 0).astype(BF16)
    fl = _dot(hb, wf_ref[...]) + bf_ref[...]
    log_f = jnp.minimum(fl, 0.0) - jnp.log(1.0 + jnp.exp(-jnp.abs(fl)))
    tri = (lax.broadcasted_iota(jnp.int32, (tm, tm), 0) >= lax.broadcasted_iota(jnp.int32, (tm, tm), 1)).astype(F32)
    dcum = carry[...] + _dot_hi(tri, log_f)
    carry[...] = dcum[tm - 1:tm, :]
    rest = -dcum * LOG2_E
    pieces = []
    for _ in range(N_BIAS):
        piece = rest.astype(BF16)
        pieces.append(piece)
        rest = rest - piece.astype(F32)
    pieces = jnp.concatenate(pieces, axis=1)
    lane = lax.broadcasted_iota(jnp.int32, (tm, LANES), 1)
    for hd in range(N_HEADS):
        tile = k[:, (hd // 2) * LANES:(hd // 2 + 1) * LANES]
        if hd % 2:
            tile = pltpu.roll(tile, HEAD_DIM, 1)
        bias = _dot(pieces, sel_ref[hd])
        k_out[0, hd] = jnp.where(lane < HEAD_DIM, tile, bias).astype(BF16)


def _shared_kv(x, gn, w_kvf, b_f, k_norm, tm=256):
    B, S, D = x.shape
    wk = w_kvf[:, :D].astype(BF16)
    wvt = w_kvf[:, D:2 * D].T.astype(BF16)
    wf = jnp.zeros((D, LANES), F32).at[:, :N_HEADS].set(w_kvf[:, 2 * D:]).astype(BF16)
    bf = jnp.zeros((1, LANES), F32).at[0, :N_HEADS].set(b_f)
    kn = jnp.tile(k_norm, N_HEADS).reshape(1, D)
    hh = jnp.arange(N_HEADS)[:, None, None]
    rr = jnp.arange(N_BIAS * LANES)[None, :, None]
    cc = jnp.arange(LANES)[None, None, :]
    sel = ((rr % LANES == hh) & (cc == HEAD_DIM + rr // LANES)).astype(BF16)
    tok = pl.BlockSpec((1, tm, D), lambda b, s: (b, s, 0))
    full = lambda a: pl.BlockSpec(a.shape, lambda b, s: (0,) * a.ndim)
    ins = [x, gn.reshape(1, D), wk, wvt, wf, bf, kn, sel]
    return pl.pallas_call(
        _kv_kernel,
        grid=(B, S // tm),
        in_specs=[tok] + [full(a) for a in ins[1:]],
        out_specs=[pl.BlockSpec((1, N_HEADS, tm, LANES), lambda b, s: (b, 0, s, 0)),
                   pl.BlockSpec((1, N_HEADS, V_ROWS, tm), lambda b, s: (b, 0, 0, s))],
        out_shape=[jax.ShapeDtypeStruct((B, N_HEADS, S, LANES), BF16),
                   jax.ShapeDtypeStruct((B, N_HEADS, V_ROWS, S), BF16)],
        scratch_shapes=[pltpu.VMEM((1, LANES), F32)],
        compiler_params=_params("arbitrary", "arbitrary"),
        name="shared_kv",
    )(*ins)


def _qg_kernel(x_ref, gn_ref, wqt_ref, wgt_ref, qn_ref, q_out, og_out):
    tm = x_ref.shape[1]
    hb = _rms(x_ref[0], gn_ref[...]).astype(BF16)
    q = _dot_nt(wqt_ref[...], hb).reshape(N_HEADS, HEAD_DIM, tm)
    q = q * lax.rsqrt(jnp.mean(q * q, axis=1, keepdims=True) + RMS_EPS) * qn_ref[...]
    q_out[0, :, :HEAD_DIM, :] = (q * (HEAD_DIM ** -0.5 * LOG2_E)).astype(BF16)
    row = lax.broadcasted_iota(jnp.int32, (N_HEADS, LANES - HEAD_DIM, tm), 1)
    q_out[0, :, HEAD_DIM:, :] = (row < N_BIAS).astype(BF16)
    og_out[0] = _dot_nt(wgt_ref[...], hb).reshape(N_HEADS, HEAD_DIM, tm).astype(BF16)


def _qg_proj(x, gn, w_qg, q_norm, tm=512):
    B, S, D = x.shape
    ins = [x, gn.reshape(1, D), w_qg[:, :D].T.astype(BF16), w_qg[:, D:].T.astype(BF16),
           q_norm.reshape(1, HEAD_DIM, 1)]
    const = lambda a: pl.BlockSpec(a.shape, lambda b, s: (0,) * a.ndim)
    return pl.pallas_call(
        _qg_kernel,
        grid=(B, S // tm),
        in_specs=[pl.BlockSpec((1, tm, D), lambda b, s: (b, s, 0))] + [const(a) for a in ins[1:]],
        out_specs=[pl.BlockSpec((1, N_HEADS, LANES, tm), lambda b, s: (b, 0, 0, s)),
                   pl.BlockSpec((1, N_HEADS, HEAD_DIM, tm), lambda b, s: (b, 0, 0, s))],
        out_shape=[jax.ShapeDtypeStruct((B, N_HEADS, LANES, S), BF16),
                   jax.ShapeDtypeStruct((B, N_HEADS, HEAD_DIM, S), BF16)],
        compiler_params=_params("arbitrary", "arbitrary"),
        name="fox_qg",
    )(*ins)


def _proj_res_t_kernel(a_ref, gate_ref, w_ref, res_ref, o_ref):
    tm = res_ref.shape[1]
    a = (a_ref[0].astype(F32) * _sigmoid(gate_ref[0].astype(F32))).astype(BF16)
    o_ref[0] = res_ref[0] + _dot_tn(a.reshape(D_MODEL, tm), w_ref[...])


def _proj_res_t(a_t, gate_t, w, res, tm=512):
    B, S, D = res.shape
    head_t = pl.BlockSpec((1, N_HEADS, HEAD_DIM, tm), lambda b, s: (b, 0, 0, s))
    tok = pl.BlockSpec((1, tm, D), lambda b, s: (b, s, 0))
    return pl.pallas_call(
        _proj_res_t_kernel,
        grid=(B, S // tm),
        in_specs=[head_t, head_t, pl.BlockSpec((D, D), lambda b, s: (0, 0)), tok],
        out_specs=tok,
        out_shape=jax.ShapeDtypeStruct((B, S, D), F32),
        compiler_params=_params("arbitrary", "arbitrary"),
        name="proj_res_gated",
    )(a_t, gate_t, w.astype(BF16), res)


def _attn_kernel(qt_ref, k_ref, vt_ref, o_ref):
    tq, tk = ATT_TQ, ATT_TK
    i = pl.program_id(2)
    hs = range(ATT_HEADS)
    qt = [qt_ref[0, h] for h in hs]
    q_pos = i * tq + lax.broadcasted_iota(jnp.int32, (tk, tq), 1)
    k_off = lax.broadcasted_iota(jnp.int32, (tk, tq), 0)

    def step(j, carry, masked):
        off = pl.multiple_of(j * tk, tk)
        s = [_dot(k_ref[0, h, pl.ds(off, tk), :], qt[h]) for h in hs]
        if masked:
            keep = k_off + off <= q_pos
            s = [jnp.where(keep, x, -jnp.inf) for x in s]
        m_new = [jnp.maximum(carry[h][0], jnp.max(s[h], axis=0, keepdims=True)) for h in hs]
        alpha = [jnp.exp2(carry[h][0] - m_new[h]) for h in hs]
        p = [jnp.exp2(s[h] - m_new[h]).astype(BF16) for h in hs]
        pv = [_dot(vt_ref[0, h, :, pl.ds(off, tk)], p[h]) for h in hs]
        return tuple((m_new[h], alpha[h] * carry[h][1] + pv[h]) for h in hs)

    init = tuple((jnp.full((1, tq), -jnp.inf, F32), jnp.zeros((V_ROWS, tq), F32)) for _ in hs)
    carry = lax.fori_loop(0, i, lambda j, c: step(j, c, False), init)
    carry = step(i, carry, True)
    for h in hs:
        acc = carry[h][1]
        o_ref[0, h] = (acc[:HEAD_DIM] / acc[HEAD_DIM:HEAD_DIM + 1]).astype(BF16)


def _fox_attn(qt, k_aug, vt):
    B, H, _, S = qt.shape
    g = ATT_HEADS
    return pl.pallas_call(
        _attn_kernel,
        grid=(B, H // g, S // ATT_TQ),
        in_specs=[pl.BlockSpec((1, g, LANES, ATT_TQ), lambda b, h, i: (b, h, 0, i)),
                  pl.BlockSpec((1, g, S, LANES), lambda b, h, i: (b, h, 0, 0)),
                  pl.BlockSpec((1, g, V_ROWS, S), lambda b, h, i: (b, h, 0, 0))],
        out_specs=pl.BlockSpec((1, g, HEAD_DIM, ATT_TQ), lambda b, h, i: (b, h, 0, i)),
        out_shape=jax.ShapeDtypeStruct((B, H, HEAD_DIM, S), BF16),
        compiler_params=_params("arbitrary", "arbitrary", "arbitrary"),
        name="fox_attn",
    )(qt, k_aug, vt)


def kernel(x, norm_mix, norm_ffn, rk_mu, rk_w_r, rk_w_k, rk_w_v, rk_w_o, rk_w0, rk_w1, rk_w2, rk_a0, rk_a1, rk_a2, rk_v0, rk_v1, rk_v2, rk_g1, rk_g2, rk_k_k, rk_k_a, rk_r_k, rk_lnx_w, rk_lnx_b, kv_norm, w_kvf, b_f, k_norm, fx_w_qg, fx_q_norm, fx_w_o, ffn_w1, ffn_w3, ffn_w2, moe_router, moe_w1, moe_w3, moe_w2):
    B, S, D = x.shape
    n = B * S
    depth = norm_mix.shape[0]
    n_rwkv = rk_mu.shape[0]
    v_first = None
    kv = None
    for l in range(depth):
        if l < n_rwkv:
            vres = None if l == 0 else (rk_v0[l - 1], rk_v1[l - 1], rk_v2[l - 1])
            r, lw, k, v, a, g = _rwkv_proj(
                x, norm_mix[l], rk_mu[l], rk_w_r[l], rk_w_k[l], rk_w_v[l], rk_w0[l], rk_w1[l], rk_w2[l],
                rk_a0[l], rk_a1[l], rk_a2[l], rk_g1[l], rk_g2[l], vres, v_first)
            if l == 0:
                v_first = v
            z = _wkv(r, lw, k, v, a, g, rk_k_k[l], rk_k_a[l], rk_r_k[l], rk_lnx_w[l], rk_lnx_b[l])
            x2 = _proj_res(z.reshape(n, D), rk_w_o[l], x.reshape(n, D))
        else:
            if kv is None:
                kv = _shared_kv(x, kv_norm, w_kvf, b_f, k_norm)
            j = l - n_rwkv
            q_t, og_t = _qg_proj(x, norm_mix[l], fx_w_qg[j], fx_q_norm[j])
            o_t = _fox_attn(q_t, *kv)
            x2 = _proj_res_t(o_t, og_t, fx_w_o[j], x).reshape(n, D)
        i = l // 2
        if l % 2 == 0:
            x2 = _ffn(x2, norm_ffn[l], ffn_w1[i], ffn_w3[i], ffn_w2[i])
        else:
            x2 = _moe(x2, norm_ffn[l], moe_router[i], moe_w1[i], moe_w3[i], moe_w2[i])
        x = x2.reshape(B, S, D)
    return x
```

```python
import functools

import jax
import jax.numpy as jnp
from jax import lax
from jax.experimental import pallas as pl
from jax.experimental.pallas import tpu as pltpu

F32 = jnp.float32
BF16 = jnp.bfloat16
HI = lax.Precision.HIGHEST

D_MODEL = 1024
HEAD_DIM = 64
N_HEADS = D_MODEL // HEAD_DIM
LANES = 128
HEAD_PAIRS = D_MODEL // LANES
N_EXPERTS = 8
LOG2_E = 1.4426950408889634
N_BIAS = 3
V_ROWS = HEAD_DIM + 16
RMS_EPS = 1e-6
GN_EPS = 64e-5
VMEM_LIMIT_BYTES = 56 * 1024 * 1024

WKV_CHUNK = 64
WKV_TILE = 512
WKV_PAIRS = 2
ATT_TQ = 512
ATT_TK = 512
ATT_HEADS = 4
MOE_BM = 1024
MOE_TF = 512


def _params(*sem):
    return pltpu.CompilerParams(dimension_semantics=sem, vmem_limit_bytes=VMEM_LIMIT_BYTES)


def _dot(a, b):
    return jnp.dot(a, b, preferred_element_type=F32)


def _dot_hi(a, b):
    return jnp.dot(a, b, preferred_element_type=F32, precision=HI)


def _dot_nt(a, b):
    return lax.dot_general(a, b, (((1,), (1,)), ((), ())), preferred_element_type=F32)


def _dot_tn(a, b):
    return lax.dot_general(a, b, (((0,), (0,)), ((), ())), preferred_element_type=F32)


def _rms(x, g):
    return x * lax.rsqrt(jnp.mean(x * x, axis=-1, keepdims=True) + RMS_EPS) * g


def _sigmoid(x):
    return 1.0 / (1.0 + jnp.exp(-x))


def _pair_ones():
    ri = lax.broadcasted_iota(jnp.int32, (LANES, LANES), 0)
    ci = lax.broadcasted_iota(jnp.int32, (LANES, LANES), 1)
    return ((ri < HEAD_DIM) == (ci < HEAD_DIM)).astype(F32)


def _head_rms(x, gain_row):
    ones = _pair_ones()
    parts = []
    for p in range(HEAD_PAIRS):
        xs = x[:, p * LANES:(p + 1) * LANES]
        ms = _dot_hi(xs * xs, ones) * (1.0 / HEAD_DIM)
        parts.append(xs * lax.rsqrt(ms + RMS_EPS))
    return jnp.concatenate(parts, axis=1) * gain_row


def _rwkv_proj_kernel(has_vres, *refs):
    if has_vres:
        (x_ref, gn_ref, mu_ref, wr_ref, wk_ref, wv_ref, w0_ref, w1_ref, w2_ref, a0_ref, a1_ref,
         a2_ref, g1_ref, g2_ref, v0_ref, v1_ref, v2_ref, vf_ref,
         r_out, lw_out, k_out, v_out, a_out, g_out, carry) = refs
    else:
        (x_ref, gn_ref, mu_ref, wr_ref, wk_ref, wv_ref, w0_ref, w1_ref, w2_ref, a0_ref, a1_ref,
         a2_ref, g1_ref, g2_ref,
         r_out, lw_out, k_out, v_out, a_out, g_out, carry) = refs
    s = pl.program_id(1)
    tm = x_ref.shape[1]

    @pl.when(s == 0)
    def _():
        carry[...] = jnp.zeros_like(carry)

    h = _rms(x_ref[0], gn_ref[...])
    prev_last = carry[...]
    carry[...] = h[tm - 1:tm, :]
    row = lax.broadcasted_iota(jnp.int32, (tm, 1), 0)
    xx = jnp.where(row == 0, prev_last, pltpu.roll(h, 1, 0)) - h

    def mix(i):
        return (h + xx * mu_ref[i:i + 1, :]).astype(BF16)

    xr, xw, xk, xv, xa, xg = [mix(i) for i in range(6)]
    r_out[0] = _dot(xr, wr_ref[...]).astype(BF16)
    k_out[0] = _dot(xk, wk_ref[...]).astype(BF16)
    v = _dot(xv, wv_ref[...])
    wl = w0_ref[...] + _dot(jnp.tanh(_dot(xw, w1_ref[...])).astype(BF16), w2_ref[...])
    w = -(jnp.maximum(-wl, 0.0) + jnp.log(1.0 + jnp.exp(-jnp.abs(wl)))) - 0.5
    lw_out[0] = -jnp.exp(w)
    a_out[0] = _sigmoid(a0_ref[...] + _dot(_dot(xa, a1_ref[...]).astype(BF16), a2_ref[...])).astype(BF16)
    g_out[0] = _dot(_sigmoid(_dot(xg, g1_ref[...])).astype(BF16), g2_ref[...]).astype(BF16)
    if has_vres:
        gate = _sigmoid(v0_ref[...] + _dot(_dot(xv, v1_ref[...]).astype(BF16), v2_ref[...]))
        v = v + (vf_ref[0].astype(F32) - v) * gate
    v_out[0] = v.astype(BF16)


def _rwkv_proj(x, gn, mu, wr, wk, wv, w0, w1, w2, a0, a1, a2, g1, g2, vres, v_first, tm=512):
    B, S, D = x.shape
    has_vres = vres is not None
    row = lambda a: a.reshape(1, -1)
    full = lambda a: pl.BlockSpec(a.shape, lambda b, s: (0,) * a.ndim, pipeline_mode=pl.Buffered(1))
    tok = pl.BlockSpec((1, tm, D), lambda b, s: (b, s, 0))
    ins = [x, row(gn), mu, wr.astype(BF16), wk.astype(BF16), wv.astype(BF16), row(w0),
           w1.astype(BF16), w2.astype(BF16), row(a0), a1.astype(BF16), a2.astype(BF16),
           g1.astype(BF16), g2.astype(BF16)]
    if has_vres:
        v0, v1, v2 = vres
        ins += [row(v0), v1.astype(BF16), v2.astype(BF16)]
    specs = [tok] + [full(a) for a in ins[1:]]
    if has_vres:
        ins.append(v_first)
        specs.append(tok)
    out_shape = [jax.ShapeDtypeStruct((B, S, D), dt) for dt in (BF16, F32, BF16, BF16, BF16, BF16)]
    return pl.pallas_call(
        functools.partial(_rwkv_proj_kernel, has_vres),
        grid=(B, S // tm),
        in_specs=specs,
        out_specs=[tok] * 6,
        out_shape=out_shape,
        scratch_shapes=[pltpu.VMEM((1, D), F32)],
        compiler_params=_params("arbitrary", "arbitrary"),
        name="rwkv_proj_vres" if has_vres else "rwkv_proj",
    )(*ins)


def _head_sum(x, ones_b):
    hi = x.astype(BF16)
    lo = (x - hi.astype(F32)).astype(BF16)
    return _dot(hi, ones_b) + _dot(lo, ones_b)


def _wkv_kernel(r_ref, lw_ref, k_ref, v_ref, a_ref, g_ref, kk_ref, ka_ref, rk_ref, lnw_ref, lnb_ref,
                z_ref, state):
    C = WKV_CHUNK
    P = 2 * C
    tt = r_ref.shape[1]
    nc = tt // C

    @pl.when(pl.program_id(2) == 0)
    def _():
        state[...] = jnp.zeros_like(state)

    lane = lax.broadcasted_iota(jnp.int32, (C, LANES), 1)
    m0 = (lane < HEAD_DIM).astype(F32)
    m1 = 1.0 - m0
    ri = lax.broadcasted_iota(jnp.int32, (P, P), 0)
    ci = lax.broadcasted_iota(jnp.int32, (P, P), 1)
    strict = ri > ci
    incl = ri >= ci
    eye = (ri == ci).astype(F32)
    tri = (lax.broadcasted_iota(jnp.int32, (C, C), 0) >= lax.broadcasted_iota(jnp.int32, (C, C), 1)).astype(F32)
    ones_b = _pair_ones().astype(BF16)

    def stack(x):
        return jnp.concatenate([x * m0, x * m1], axis=0).astype(BF16)

    nq = r_ref.shape[2] // LANES
    cs = range(nc)
    pre = []
    for q in range(nq):
        ls = slice(q * LANES, (q + 1) * LANES)
        r = r_ref[0, :, ls].astype(F32)
        k = k_ref[0, :, ls].astype(F32)
        v = v_ref[0, :, ls].astype(F32)
        a = a_ref[0, :, ls].astype(F32)
        lw = lw_ref[0, :, ls]
        kk = k * kk_ref[:, ls]
        kk = kk * lax.rsqrt(jnp.maximum(_head_sum(kk * kk, ones_b), 1e-24))
        k2 = k * (1.0 + (a - 1.0) * ka_ref[:, ls])
        bv = kk * a
        cum_c = _dot_hi(tri, jnp.concatenate([lw[c * C:(c + 1) * C] for c in cs], axis=1))
        cum = jnp.concatenate([cum_c[:, c * LANES:(c + 1) * LANES] for c in cs], axis=0)
        cum_end = [cum_c[C - 1:C, c * LANES:(c + 1) * LANES] for c in cs]
        to_end = jnp.concatenate([jnp.broadcast_to(ce, (C, LANES)) for ce in cum_end], axis=0) - cum
        g_inv = jnp.exp(-cum)
        g_end = jnp.exp(to_end)
        pre.append(dict(r=r, k2=k2, v=v, cum_end=cum_end, a_t=-kk * jnp.exp(cum - lw), r_t=r * jnp.exp(cum),
                        b_t=bv * g_inv, k_t=k2 * g_inv, b_h=bv * g_end, k_h=k2 * g_end))

    units = [(q, c) for c in cs for q in range(nq)]
    us = range(len(units))

    def stacks(name):
        return [stack(pre[q][name][c * C:(c + 1) * C]) for q, c in units]

    a_s, r_s, b_s, k_s, v_s, bh_s, kh_s = (stacks(x) for x in ("a_t", "r_t", "b_t", "k_t", "v", "b_h", "k_h"))
    cat0 = lambda x, y: jnp.concatenate([x, y], axis=0)
    cat1 = lambda x, y: jnp.concatenate([x, y], axis=1)
    gram = [_dot_nt(cat0(a_s[u], r_s[u]), cat0(b_s[u], k_s[u])) for u in us]
    n_ab = [jnp.where(strict, g[:P, :P], 0.0) for g in gram]
    a_ak = [jnp.where(strict, g[:P, P:], 0.0).astype(BF16) for g in gram]
    a_rb = [jnp.where(incl, g[P:, :P], 0.0).astype(BF16) for g in gram]
    a_rk = [jnp.where(incl, g[P:, P:], 0.0).astype(BF16) for g in gram]
    akv = [_dot(a_ak[u], v_s[u]).astype(BF16) for u in us]
    pw = [n.astype(BF16) for n in n_ab]
    inv = [eye + n for n in n_ab]
    pw = [_dot(p, p).astype(BF16) for p in pw]
    for _ in range(4):
        both = [_dot(cat0(pw[u], inv[u].astype(BF16)), pw[u]) for u in us]
        pw = [x[:P].astype(BF16) for x in both]
        inv = [inv[u] + both[u][P:] for u in us]
    inv = [(inv[u] + _dot(inv[u].astype(BF16), pw[u])).astype(BF16) for u in us]
    wu = [_dot(inv[u], cat1(a_s[u], akv[u])).astype(BF16) for u in us]
    rb = [_dot(a_rb[u], wu[u]) for u in us]
    q_h = [(r_s[u].astype(F32) + rb[u][:, :LANES]).astype(BF16) for u in us]
    y0 = [rb[u][:, LANES:] + _dot(a_rk[u], v_s[u]) for u in us]
    m_st = [_dot_tn(wu[u][:, :LANES], bh_s[u]).astype(BF16) for u in us]
    z_st = [_dot_tn(cat0(wu[u][:, LANES:], v_s[u]), cat0(bh_s[u], kh_s[u])) for u in us]

    s = [state[q] for q in range(nq)]
    ys = [[] for _ in range(nq)]
    for c in cs:
        for q in range(nq):
            u = c * nq + q
            sb = s[q].astype(BF16)
            y_s = _dot_nt(q_h[u], sb) + y0[u]
            ys[q].append(y_s[:C] + y_s[C:])
            s[q] = s[q] * jnp.exp(pre[q]["cum_end"][c]) + _dot(sb, m_st[u]) + z_st[u]
    for q in range(nq):
        state[q] = s[q]

    for q in range(nq):
        ls = slice(q * LANES, (q + 1) * LANES)
        y = jnp.concatenate(ys[q], axis=0)
        mean = _head_sum(y, ones_b) * (1.0 / HEAD_DIM)
        yc = y - mean
        var = _head_sum(yc * yc, ones_b) * (1.0 / HEAD_DIM)
        yn = yc * lax.rsqrt(var + GN_EPS) * lnw_ref[:, ls] + lnb_ref[:, ls]
        bonus = _head_sum(pre[q]["r"] * pre[q]["k2"] * rk_ref[:, ls], ones_b) * pre[q]["v"]
        z_ref[0, :, ls] = ((yn + bonus) * g_ref[0, :, ls].astype(F32)).astype(BF16)


def _wkv(r, lw, k, v, a, g, k_k, k_a, r_k, ln_w, ln_b):
    B, S, D = r.shape
    tt = min(WKV_TILE, S)
    width = WKV_PAIRS * LANES
    tok = pl.BlockSpec((1, tt, width), lambda b, p, t: (b, t, p))
    par = pl.BlockSpec((1, width), lambda b, p, t: (0, p))
    row = lambda x: x.reshape(1, D)
    return pl.pallas_call(
        _wkv_kernel,
        grid=(B, D // width, S // tt),
        in_specs=[tok] * 6 + [par] * 5,
        out_specs=tok,
        out_shape=jax.ShapeDtypeStruct((B, S, D), BF16),
        scratch_shapes=[pltpu.VMEM((WKV_PAIRS, LANES, LANES), F32)],
        compiler_params=_params("arbitrary", "arbitrary", "arbitrary"),
        name="wkv7",
    )(r, lw, k, v, a, g, row(k_k), row(k_a), row(r_k), row(ln_w), row(ln_b))


def _proj_res_kernel(a_ref, w_ref, res_ref, o_ref):
    o_ref[...] = res_ref[...] + _dot(a_ref[...], w_ref[...])


def _proj_res(a, w, res, tm=512):
    n, kdim = a.shape
    d = w.shape[1]
    return pl.pallas_call(
        _proj_res_kernel,
        grid=(n // tm,),
        in_specs=[pl.BlockSpec((tm, kdim), lambda i: (i, 0)), pl.BlockSpec((kdim, d), lambda i: (0, 0)),
                  pl.BlockSpec((tm, d), lambda i: (i, 0))],
        out_specs=pl.BlockSpec((tm, d), lambda i: (i, 0)),
        out_shape=jax.ShapeDtypeStruct((n, d), F32),
        compiler_params=_params("arbitrary"),
        name="proj_res",
    )(a, w.astype(BF16), res)


def _ffn_kernel(x_ref, gn_ref, w1_ref, w3_ref, w2_ref, o_ref, h_ref, acc_ref):
    f = pl.program_id(1)

    @pl.when(f == 0)
    def _():
        h_ref[...] = _rms(x_ref[...], gn_ref[...]).astype(BF16)

    h = h_ref[...]
    h1 = _dot(h, w1_ref[...])
    act = (h1 * _sigmoid(h1) * _dot(h, w3_ref[...])).astype(BF16)
    part = _dot(act, w2_ref[...])

    @pl.when(f == 0)
    def _():
        acc_ref[...] = part

    @pl.when(f > 0)
    def _():
        acc_ref[...] += part

    @pl.when(f == pl.num_programs(1) - 1)
    def _():
        o_ref[...] = x_ref[...] + acc_ref[...]


def _ffn(x, gn, w1, w3, w2, tm=512, tf=1408):
    n, d = x.shape
    fdim = w1.shape[1]
    return pl.pallas_call(
        _ffn_kernel,
        grid=(n // tm, fdim // tf),
        in_specs=[pl.BlockSpec((tm, d), lambda i, f: (i, 0)),
                  pl.BlockSpec((1, d), lambda i, f: (0, 0)),
                  pl.BlockSpec((d, tf), lambda i, f: (0, f)),
                  pl.BlockSpec((d, tf), lambda i, f: (0, f)),
                  pl.BlockSpec((tf, d), lambda i, f: (f, 0))],
        out_specs=pl.BlockSpec((tm, d), lambda i, f: (i, 0)),
        out_shape=jax.ShapeDtypeStruct((n, d), F32),
        scratch_shapes=[pltpu.VMEM((tm, d), BF16), pltpu.VMEM((tm, d), F32)],
        compiler_params=_params("arbitrary", "arbitrary"),
        name="ffn_dense",
    )(x, gn.reshape(1, d), w1.astype(BF16), w3.astype(BF16), w2.astype(BF16))


def _router_kernel(x_ref, gn_ref, wr_ref, h_ref, route_ref, cnt_ref, carry):
    tm = x_ref.shape[0]

    @pl.when(pl.program_id(0) == 0)
    def _():
        carry[...] = jnp.zeros_like(carry)

    h = _rms(x_ref[...], gn_ref[...])
    h_ref[...] = h
    logits = _dot_hi(h, wr_ref[...])
    lane = lax.broadcasted_iota(jnp.int32, logits.shape, 1)
    neg = jnp.float32(-jnp.inf)
    logits = jnp.where(lane < N_EXPERTS, logits, neg)
    m1 = jnp.max(logits, axis=-1, keepdims=True)
    i1 = jnp.min(jnp.where(logits == m1, lane, LANES), axis=-1, keepdims=True)
    rest = jnp.where(lane == i1, neg, logits)
    m2 = jnp.max(rest, axis=-1, keepdims=True)
    i2 = jnp.min(jnp.where(rest == m2, lane, LANES), axis=-1, keepdims=True)
    e2 = jnp.exp(m2 - m1)
    p1 = 1.0 / (1.0 + e2)
    p2 = e2 * p1
    oh1 = (lane == i1).astype(F32)
    oh2 = (lane == i2).astype(F32)
    cnt = oh1 + oh2
    earlier = (lax.broadcasted_iota(jnp.int32, (tm, tm), 0) > lax.broadcasted_iota(jnp.int32, (tm, tm), 1))
    before = carry[...] + _dot(earlier.astype(BF16), cnt.astype(BF16))
    r1 = jnp.sum(oh1 * before, axis=-1, keepdims=True)
    r2 = jnp.sum(oh2 * before, axis=-1, keepdims=True)
    total = carry[...] + jnp.sum(cnt, axis=0, keepdims=True)
    carry[...] = total
    cnt_ref[...] = total
    cols = (i1.astype(F32), i2.astype(F32), p1, p2, r1, r2)
    route = jnp.zeros(logits.shape, F32)
    for c, val in enumerate(cols):
        route = jnp.where(lane == c, val, route)
    route_ref[...] = route


def _router(x, gn, w_router, tm=512):
    n, d = x.shape
    wr = jnp.zeros((d, LANES), F32).at[:, :N_EXPERTS].set(w_router)
    return pl.pallas_call(
        _router_kernel,
        grid=(n // tm,),
        in_specs=[pl.BlockSpec((tm, d), lambda i: (i, 0)),
                  pl.BlockSpec((1, d), lambda i: (0, 0)),
                  pl.BlockSpec((d, LANES), lambda i: (0, 0))],
        out_specs=[pl.BlockSpec((tm, d), lambda i: (i, 0)), pl.BlockSpec((tm, LANES), lambda i: (i, 0)),
                   pl.BlockSpec((1, LANES), lambda i: (0, 0))],
        out_shape=[jax.ShapeDtypeStruct((n, d), F32), jax.ShapeDtypeStruct((n, LANES), F32),
                   jax.ShapeDtypeStruct((1, LANES), F32)],
        scratch_shapes=[pltpu.VMEM((1, LANES), F32)],
        compiler_params=_params("arbitrary"),
        name="moe_router",
    )(x, gn.reshape(1, d), wr)


def _dispatch_kernel(dest_ref, pend_ref, h_ref, xs_ref, zeros, sem, zsem):
    i = pl.program_id(0)
    tm = h_ref.shape[0]
    n = pl.num_programs(0) * tm

    @pl.when(i == 0)
    def _():
        zeros[...] = jnp.zeros_like(zeros)

        def tail(e):
            start = pl.multiple_of(pend_ref[e] - MOE_BM, MOE_BM)
            return pltpu.make_async_copy(zeros, xs_ref.at[pl.ds(start, MOE_BM)], zsem)

        def nonempty(e):
            return pend_ref[e] > (pend_ref[e - 1] if e else 0)

        for e in range(N_EXPERTS):
            pl.when(nonempty(e))(lambda e=e: tail(e).start())
        for e in range(N_EXPERTS):
            pl.when(nonempty(e))(lambda e=e: tail(e).wait())

        def spare(b):
            return pltpu.make_async_copy(zeros, xs_ref.at[pl.ds(pl.multiple_of(b * MOE_BM, MOE_BM), MOE_BM)], zsem)

        first_spare = pend_ref[N_EXPERTS - 1] // MOE_BM
        n_blocks = xs_ref.shape[0] // MOE_BM
        lax.fori_loop(first_spare, n_blocks, lambda b, c: (spare(b).start(), c)[1], 0)
        lax.fori_loop(first_spare, n_blocks, lambda b, c: (spare(b).wait(), c)[1], 0)

    def row_copy(j, k):
        return pltpu.make_async_copy(h_ref.at[pl.ds(j, 1)],
                                     xs_ref.at[pl.ds(dest_ref[k * n + i * tm + j], 1)], sem)

    def start(j, c):
        row_copy(j, 0).start()
        row_copy(j, 1).start()
        return c

    def wait(j, c):
        row_copy(j, 0).wait()
        row_copy(j, 1).wait()
        return c

    lax.fori_loop(0, tm, start, 0, unroll=8)
    lax.fori_loop(0, tm, wait, 0, unroll=8)


def _dispatch(h, dest, pend, cap, tm=512):
    n, d = h.shape
    return pl.pallas_call(
        _dispatch_kernel,
        grid_spec=pltpu.PrefetchScalarGridSpec(
            num_scalar_prefetch=2,
            grid=(n // tm,),
            in_specs=[pl.BlockSpec((tm, d), lambda i, dest, pend: (i, 0))],
            out_specs=pl.BlockSpec(memory_space=pl.ANY),
            scratch_shapes=[pltpu.VMEM((MOE_BM, d), F32), pltpu.SemaphoreType.DMA(()),
                            pltpu.SemaphoreType.DMA(())],
        ),
        out_shape=jax.ShapeDtypeStruct((cap, d), F32),
        compiler_params=_params("arbitrary"),
        name="moe_dispatch",
    )(dest, pend, h)


def _experts_kernel(be_ref, nu_ref, x_ref, w1_ref, w3_ref, w2_ref, o_ref, xb_ref, acc_ref):
    i, f = pl.program_id(0), pl.program_id(1)
    used = i < nu_ref[0]

    @pl.when(jnp.logical_and(used, f == 0))
    def _():
        xb_ref[...] = x_ref[...].astype(BF16)

    @pl.when(used)
    def _():
        x = xb_ref[...]
        h1 = _dot(x, w1_ref[0])
        act = (h1 * _sigmoid(h1) * _dot(x, w3_ref[0])).astype(BF16)
        part = _dot(act, w2_ref[0])

        @pl.when(f == 0)
        def _():
            acc_ref[...] = part

        @pl.when(f > 0)
        def _():
            acc_ref[...] += part

    @pl.when(f == pl.num_programs(1) - 1)
    def _():
        o_ref[...] = jnp.where(used, acc_ref[...], 0.0)


def _experts(xs, blk_e, n_used, w1, w3, w2, tf=MOE_TF):
    cap, d = xs.shape
    edim = w1.shape[2]
    nb = cap // MOE_BM
    return pl.pallas_call(
        _experts_kernel,
        grid_spec=pltpu.PrefetchScalarGridSpec(
            num_scalar_prefetch=2,
            grid=(nb, edim // tf),
            in_specs=[pl.BlockSpec((MOE_BM, d), lambda i, f, be, nu: (jnp.minimum(i, nu[0] - 1), 0)),
                      pl.BlockSpec((1, d, tf), lambda i, f, be, nu: (be[i], 0, f)),
                      pl.BlockSpec((1, d, tf), lambda i, f, be, nu: (be[i], 0, f)),
                      pl.BlockSpec((1, tf, d), lambda i, f, be, nu: (be[i], f, 0))],
            out_specs=pl.BlockSpec((MOE_BM, d), lambda i, f, be, nu: (i, 0)),
            scratch_shapes=[pltpu.VMEM((MOE_BM, d), BF16), pltpu.VMEM((MOE_BM, d), F32)],
        ),
        out_shape=jax.ShapeDtypeStruct((cap, d), F32),
        compiler_params=_params("arbitrary", "arbitrary"),
        name="moe_experts",
    )(blk_e, n_used, xs, w1.astype(BF16), w3.astype(BF16), w2.astype(BF16))


def _combine_kernel(dest_ref, x_ref, route_ref, yb_ref, o_ref, buf, sem):
    i = pl.program_id(0)
    nt = pl.num_programs(0)
    tm = x_ref.shape[0]
    n = nt * tm

    def row_copy(tile, slot, j, k):
        return pltpu.make_async_copy(yb_ref.at[pl.ds(dest_ref[k * n + tile * tm + j], 1)],
                                     buf.at[slot, k, pl.ds(j, 1)], sem.at[slot])

    def fetch(tile, slot):
        def body(j, c):
            row_copy(tile, slot, j, 0).start()
            row_copy(tile, slot, j, 1).start()
            return c
        lax.fori_loop(0, tm, body, 0, unroll=8)

    def drain(tile, slot):
        def body(j, c):
            row_copy(tile, slot, j, 0).wait()
            row_copy(tile, slot, j, 1).wait()
            return c
        lax.fori_loop(0, tm, body, 0, unroll=8)

    slot = i % 2

    @pl.when(i == 0)
    def _():
        fetch(0, 0)

    @pl.when(i + 1 < nt)
    def _():
        fetch(i + 1, 1 - slot)

    drain(i, slot)
    p1 = route_ref[:, 2:3]
    p2 = route_ref[:, 3:4]
    o_ref[...] = x_ref[...] + (p1 * buf[slot, 0] + p2 * buf[slot, 1])


def _combine(x, route, yb, dest, tm=512):
    n, d = x.shape
    return pl.pallas_call(
        _combine_kernel,
        grid_spec=pltpu.PrefetchScalarGridSpec(
            num_scalar_prefetch=1,
            grid=(n // tm,),
            in_specs=[pl.BlockSpec((tm, d), lambda i, dest: (i, 0)),
                      pl.BlockSpec((tm, LANES), lambda i, dest: (i, 0)),
                      pl.BlockSpec(memory_space=pl.ANY)],
            out_specs=pl.BlockSpec((tm, d), lambda i, dest: (i, 0)),
            scratch_shapes=[pltpu.VMEM((2, 2, tm, d), F32), pltpu.SemaphoreType.DMA((2,))],
        ),
        out_shape=jax.ShapeDtypeStruct((n, d), F32),
        compiler_params=_params("arbitrary"),
        name="moe_combine",
    )(dest, x, route, yb)


def _moe(x, gn, w_router, w1, w3, w2):
    n, d = x.shape
    h, route, cnt = _router(x, gn, w_router)
    counts = cnt[0, :N_EXPERTS].astype(jnp.int32)
    padded = ((counts + MOE_BM - 1) // MOE_BM) * MOE_BM
    pend = jnp.cumsum(padded).astype(jnp.int32)
    pstart = pend - padded
    experts = jnp.arange(N_EXPERTS, dtype=jnp.int32)

    def dest_of(e_col, r_col):
        e = route[:, e_col].astype(jnp.int32)
        start = jnp.sum(jnp.where(e[:, None] == experts[None, :], pstart[None, :], 0), axis=1)
        return start + route[:, r_col].astype(jnp.int32)

    dest = jnp.concatenate([dest_of(0, 4), dest_of(1, 5)])
    nb = -(-(2 * n) // MOE_BM) + N_EXPERTS
    blk_e = jnp.minimum(jnp.searchsorted(pend, jnp.arange(nb, dtype=jnp.int32) * MOE_BM, side="right"),
                        N_EXPERTS - 1).astype(jnp.int32)
    n_used = (pend[-1:] // MOE_BM).astype(jnp.int32)
    xs = _dispatch(h, dest, pend, nb * MOE_BM)
    yb = _experts(xs, blk_e, n_used, w1, w3, w2)
    return _combine(x, route, yb, dest)


def _kv_kernel(x_ref, gn_ref, wk_ref, wvt_ref, wf_ref, bf_ref, kn_ref, sel_ref, k_out, v_out, carry):
    s = pl.program_id(1)
    tm = x_ref.shape[1]

    @pl.when(s == 0)
    def _():
        carry[...] = jnp.zeros_like(carry)

    h = _rms(x_ref[0], gn_ref[...])
    hb = h.astype(BF16)
    k = _head_rms(_dot(hb, wk_ref[...]), kn_ref[...])
    v_out[0, :, :HEAD_DIM, :] = _dot_nt(wvt_ref[...], hb).reshape(N_HEADS, HEAD_DIM, tm).astype(BF16)
    row = lax.broadcasted_iota(jnp.int32, (N_HEADS, V_ROWS - HEAD_DIM, tm), 1)
    v_out[0, :, HEAD_DIM:, :] = (row == 0).astype(BF16)
    fl = _dot(hb, wf_ref[...]) + bf_ref[...]
    log_f = jnp.minimum(fl, 0.0) - jnp.log(1.0 + jnp.exp(-jnp.abs(fl)))
    tri = (lax.broadcasted_iota(jnp.int32, (tm, tm), 0) >= lax.broadcasted_iota(jnp.int32, (tm, tm), 1)).astype(F32)
    dcum = carry[...] + _dot_hi(tri, log_f)
    carry[...] = dcum[tm - 1:tm, :]
    rest = -dcum * LOG2_E
    pieces = []
    for _ in range(N_BIAS):
        piece = rest.astype(BF16)
        pieces.append(piece)
        rest = rest - piece.astype(F32)
    pieces = jnp.concatenate(pieces, axis=1)
    lane = lax.broadcasted_iota(jnp.int32, (tm, LANES), 1)
    for hd in range(N_HEADS):
        tile = k[:, (hd // 2) * LANES:(hd // 2 + 1) * LANES]
        if hd % 2:
            tile = pltpu.roll(tile, HEAD_DIM, 1)
        bias = _dot(pieces, sel_ref[hd])
        k_out[0, hd] = jnp.where(lane < HEAD_DIM, tile, bias).astype(BF16)


def _shared_kv(x, gn, w_kvf, b_f, k_norm, tm=256):
    B, S, D = x.shape
    wk = w_kvf[:, :D].astype(BF16)
    wvt = w_kvf[:, D:2 * D].T.astype(BF16)
    wf = jnp.zeros((D, LANES), F32).at[:, :N_HEADS].set(w_kvf[:, 2 * D:]).astype(BF16)
    bf = jnp.zeros((1, LANES), F32).at[0, :N_HEADS].set(b_f)
    kn = jnp.tile(k_norm, N_HEADS).reshape(1, D)
    hh = jnp.arange(N_HEADS)[:, None, None]
    rr = jnp.arange(N_BIAS * LANES)[None, :, None]
    cc = jnp.arange(LANES)[None, None, :]
    sel = ((rr % LANES == hh) & (cc == HEAD_DIM + rr // LANES)).astype(BF16)
    tok = pl.BlockSpec((1, tm, D), lambda b, s: (b, s, 0))
    full = lambda a: pl.BlockSpec(a.shape, lambda b, s: (0,) * a.ndim)
    ins = [x, gn.reshape(1, D), wk, wvt, wf, bf, kn, sel]
    return pl.pallas_call(
        _kv_kernel,
        grid=(B, S // tm),
        in_specs=[tok] + [full(a) for a in ins[1:]],
        out_specs=[pl.BlockSpec((1, N_HEADS, tm, LANES), lambda b, s: (b, 0, s, 0)),
                   pl.BlockSpec((1, N_HEADS, V_ROWS, tm), lambda b, s: (b, 0, 0, s))],
        out_shape=[jax.ShapeDtypeStruct((B, N_HEADS, S, LANES), BF16),
                   jax.ShapeDtypeStruct((B, N_HEADS, V_ROWS, S), BF16)],
        scratch_shapes=[pltpu.VMEM((1, LANES), F32)],
        compiler_params=_params("arbitrary", "arbitrary"),
        name="shared_kv",
    )(*ins)


def _qg_kernel(x_ref, gn_ref, wqt_ref, wgt_ref, qn_ref, q_out, og_out):
    tm = x_ref.shape[1]
    hb = _rms(x_ref[0], gn_ref[...]).astype(BF16)
    q = _dot_nt(wqt_ref[...], hb).reshape(N_HEADS, HEAD_DIM, tm)
    q = q * lax.rsqrt(jnp.mean(q * q, axis=1, keepdims=True) + RMS_EPS) * qn_ref[...]
    q_out[0, :, :HEAD_DIM, :] = (q * (HEAD_DIM ** -0.5 * LOG2_E)).astype(BF16)
    row = lax.broadcasted_iota(jnp.int32, (N_HEADS, LANES - HEAD_DIM, tm), 1)
    q_out[0, :, HEAD_DIM:, :] = (row < N_BIAS).astype(BF16)
    og_out[0] = _dot_nt(wgt_ref[...], hb).reshape(N_HEADS, HEAD_DIM, tm).astype(BF16)


def _qg_proj(x, gn, w_qg, q_norm, tm=512):
    B, S, D = x.shape
    ins = [x, gn.reshape(1, D), w_qg[:, :D].T.astype(BF16), w_qg[:, D:].T.astype(BF16),
           q_norm.reshape(1, HEAD_DIM, 1)]
    const = lambda a: pl.BlockSpec(a.shape, lambda b, s: (0,) * a.ndim)
    return pl.pallas_call(
        _qg_kernel,
        grid=(B, S // tm),
        in_specs=[pl.BlockSpec((1, tm, D), lambda b, s: (b, s, 0))] + [const(a) for a in ins[1:]],
        out_specs=[pl.BlockSpec((1, N_HEADS, LANES, tm), lambda b, s: (b, 0, 0, s)),
                   pl.BlockSpec((1, N_HEADS, HEAD_DIM, tm), lambda b, s: (b, 0, 0, s))],
        out_shape=[jax.ShapeDtypeStruct((B, N_HEADS, LANES, S), BF16),
                   jax.ShapeDtypeStruct((B, N_HEADS, HEAD_DIM, S), BF16)],
        compiler_params=_params("arbitrary", "arbitrary"),
        name="fox_qg",
    )(*ins)


def _proj_res_t_kernel(a_ref, gate_ref, w_ref, res_ref, o_ref):
    tm = res_ref.shape[1]
    a = (a_ref[0].astype(F32) * _sigmoid(gate_ref[0].astype(F32))).astype(BF16)
    o_ref[0] = res_ref[0] + _dot_tn(a.reshape(D_MODEL, tm), w_ref[...])


def _proj_res_t(a_t, gate_t, w, res, tm=512):
    B, S, D = res.shape
    head_t = pl.BlockSpec((1, N_HEADS, HEAD_DIM, tm), lambda b, s: (b, 0, 0, s))
    tok = pl.BlockSpec((1, tm, D), lambda b, s: (b, s, 0))
    return pl.pallas_call(
        _proj_res_t_kernel,
        grid=(B, S // tm),
        in_specs=[head_t, head_t, pl.BlockSpec((D, D), lambda b, s: (0, 0)), tok],
        out_specs=tok,
        out_shape=jax.ShapeDtypeStruct((B, S, D), F32),
        compiler_params=_params("arbitrary", "arbitrary"),
        name="proj_res_gated",
    )(a_t, gate_t, w.astype(BF16), res)


def _attn_kernel(qt_ref, k_ref, vt_ref, o_ref):
    tq, tk = ATT_TQ, ATT_TK
    i = pl.program_id(2)
    hs = range(ATT_HEADS)
    qt = [qt_ref[0, h] for h in hs]

    def block(carry, off, nk, q_lo, masked):
        qs = slice(q_lo, tq)
        s = [_dot(k_ref[0, h, pl.ds(off, nk), :], qt[h][:, qs]) for h in hs]
        if masked:
            shape = (nk, tq - q_lo)
            key = off + lax.broadcasted_iota(jnp.int32, shape, 0)
            qry = i * tq + q_lo + lax.broadcasted_iota(jnp.int32, shape, 1)
            s = [jnp.where(key <= qry, x, -jnp.inf) for x in s]
        m_old = [carry[h][0][:, qs] for h in hs]
        m_new = [jnp.maximum(m_old[h], jnp.max(s[h], axis=0, keepdims=True)) for h in hs]
        alpha = [jnp.exp2(m_old[h] - m_new[h]) for h in hs]
        p = [jnp.exp2(s[h] - m_new[h]).astype(BF16) for h in hs]
        pv = [_dot(vt_ref[0, h, :, pl.ds(off, nk)], p[h]) for h in hs]
        acc = [alpha[h] * carry[h][1][:, qs] + pv[h] for h in hs]
        if q_lo:
            m_new = [jnp.concatenate([carry[h][0][:, :q_lo], m_new[h]], axis=1) for h in hs]
            acc = [jnp.concatenate([carry[h][1][:, :q_lo], acc[h]], axis=1) for h in hs]
        return tuple((m_new[h], acc[h]) for h in hs)

    init = tuple((jnp.full((1, tq), -jnp.inf, F32), jnp.zeros((V_ROWS, tq), F32)) for _ in hs)
    carry = lax.fori_loop(0, i, lambda j, c: block(c, pl.multiple_of(j * tk, tk), tk, 0, False), init)
    diag = pl.multiple_of(i * tk, tk)
    carry = block(carry, diag, tk // 2, 0, True)
    carry = block(carry, diag + tk // 2, tk // 2, tq // 2, True)
    for h in hs:
        acc = carry[h][1]
        o_ref[0, h] = (acc[:HEAD_DIM] / acc[HEAD_DIM:HEAD_DIM + 1]).astype(BF16)


def _fox_attn(qt, k_aug, vt):
    B, H, _, S = qt.shape
    g = ATT_HEADS
    return pl.pallas_call(
        _attn_kernel,
        grid=(B, H // g, S // ATT_TQ),
        in_specs=[pl.BlockSpec((1, g, LANES, ATT_TQ), lambda b, h, i: (b, h, 0, i)),
                  pl.BlockSpec((1, g, S, LANES), lambda b, h, i: (b, h, 0, 0)),
                  pl.BlockSpec((1, g, V_ROWS, S), lambda b, h, i: (b, h, 0, 0))],
        out_specs=pl.BlockSpec((1, g, HEAD_DIM, ATT_TQ), lambda b, h, i: (b, h, 0, i)),
        out_shape=jax.ShapeDtypeStruct((B, H, HEAD_DIM, S), BF16),
        compiler_params=_params("arbitrary", "arbitrary", "arbitrary"),
        name="fox_attn",
    )(qt, k_aug, vt)


def kernel(x, norm_mix, norm_ffn, rk_mu, rk_w_r, rk_w_k, rk_w_v, rk_w_o, rk_w0, rk_w1, rk_w2, rk_a0, rk_a1, rk_a2, rk_v0, rk_v1, rk_v2, rk_g1, rk_g2, rk_k_k, rk_k_a, rk_r_k, rk_lnx_w, rk_lnx_b, kv_norm, w_kvf, b_f, k_norm, fx_w_qg, fx_q_norm, fx_w_o, ffn_w1, ffn_w3, ffn_w2, moe_router, moe_w1, moe_w3, moe_w2):
    B, S, D = x.shape
    n = B * S
    depth = norm_mix.shape[0]
    n_rwkv = rk_mu.shape[0]
    v_first = None
    kv = None
    for l in range(depth):
        if l < n_rwkv:
            vres = None if l == 0 else (rk_v0[l - 1], rk_v1[l - 1], rk_v2[l - 1])
            r, lw, k, v, a, g = _rwkv_proj(
                x, norm_mix[l], rk_mu[l], rk_w_r[l], rk_w_k[l], rk_w_v[l], rk_w0[l], rk_w1[l], rk_w2[l],
                rk_a0[l], rk_a1[l], rk_a2[l], rk_g1[l], rk_g2[l], vres, v_first)
            if l == 0:
                v_first = v
            z = _wkv(r, lw, k, v, a, g, rk_k_k[l], rk_k_a[l], rk_r_k[l], rk_lnx_w[l], rk_lnx_b[l])
            x2 = _proj_res(z.reshape(n, D), rk_w_o[l], x.reshape(n, D))
        else:
            if kv is None:
                kv = _shared_kv(x, kv_norm, w_kvf, b_f, k_norm)
            j = l - n_rwkv
            q_t, og_t = _qg_proj(x, norm_mix[l], fx_w_qg[j], fx_q_norm[j])
            o_t = _fox_attn(q_t, *kv)
            x2 = _proj_res_t(o_t, og_t, fx_w_o[j], x).reshape(n, D)
        i = l // 2
        if l % 2 == 0:
            x2 = _ffn(x2, norm_ffn[l], ffn_w1[i], ffn_w3[i], ffn_w2[i])
        else:
            x2 = _moe(x2, norm_ffn[l], moe_router[i], moe_w1[i], moe_w3[i], moe_w2[i])
        x = x2.reshape(B, S, D)
    return x
```

```python
import functools

import jax
import jax.numpy as jnp
from jax import lax
from jax.experimental import pallas as pl
from jax.experimental.pallas import tpu as pltpu

F32 = jnp.float32
BF16 = jnp.bfloat16
HI = lax.Precision.HIGHEST

D_MODEL = 1024
HEAD_DIM = 64
N_HEADS = D_MODEL // HEAD_DIM
LANES = 128
HEAD_PAIRS = D_MODEL // LANES
N_EXPERTS = 8
LOG2_E = 1.4426950408889634
N_BIAS = 3
V_ROWS = HEAD_DIM + 16
RMS_EPS = 1e-6
GN_EPS = 64e-5
VMEM_LIMIT_BYTES = 56 * 1024 * 1024

WKV_CHUNK = 64
WKV_TILE = 512
WKV_PAIRS = 2
ATT_TQ = 512
ATT_TK = 512
ATT_HEADS = 4
MOE_BM = 1024
MOE_TF = 1792
MOE_SLAB = 256


def _params(*sem):
    return pltpu.CompilerParams(dimension_semantics=sem, vmem_limit_bytes=VMEM_LIMIT_BYTES)


def _dot(a, b):
    return jnp.dot(a, b, preferred_element_type=F32)


def _dot_hi(a, b):
    return jnp.dot(a, b, preferred_element_type=F32, precision=HI)


def _dot_nt(a, b):
    return lax.dot_general(a, b, (((1,), (1,)), ((), ())), preferred_element_type=F32)


def _dot_tn(a, b):
    return lax.dot_general(a, b, (((0,), (0,)), ((), ())), preferred_element_type=F32)


def _rms(x, g):
    return x * lax.rsqrt(jnp.mean(x * x, axis=-1, keepdims=True) + RMS_EPS) * g


def _sigmoid(x):
    return 1.0 / (1.0 + jnp.exp(-x))


def _pair_ones():
    ri = lax.broadcasted_iota(jnp.int32, (LANES, LANES), 0)
    ci = lax.broadcasted_iota(jnp.int32, (LANES, LANES), 1)
    return ((ri < HEAD_DIM) == (ci < HEAD_DIM)).astype(F32)


def _head_rms(x, gain_row):
    ones = _pair_ones()
    parts = []
    for p in range(HEAD_PAIRS):
        xs = x[:, p * LANES:(p + 1) * LANES]
        ms = _dot_hi(xs * xs, ones) * (1.0 / HEAD_DIM)
        parts.append(xs * lax.rsqrt(ms + RMS_EPS))
    return jnp.concatenate(parts, axis=1) * gain_row


def _rwkv_proj_kernel(has_vres, *refs):
    if has_vres:
        (x_ref, gn_ref, mu_ref, wr_ref, wk_ref, wv_ref, w0_ref, w1_ref, w2_ref, a0_ref, a1_ref,
         a2_ref, g1_ref, g2_ref, v0_ref, v1_ref, v2_ref, vf_ref,
         r_out, lw_out, k_out, v_out, a_out, g_out, carry) = refs
    else:
        (x_ref, gn_ref, mu_ref, wr_ref, wk_ref, wv_ref, w0_ref, w1_ref, w2_ref, a0_ref, a1_ref,
         a2_ref, g1_ref, g2_ref,
         r_out, lw_out, k_out, v_out, a_out, g_out, carry) = refs
    s = pl.program_id(1)
    tm = x_ref.shape[1]

    @pl.when(s == 0)
    def _():
        carry[...] = jnp.zeros_like(carry)

    h = _rms(x_ref[0], gn_ref[...])
    prev_last = carry[...]
    carry[...] = h[tm - 1:tm, :]
    row = lax.broadcasted_iota(jnp.int32, (tm, 1), 0)
    xx = jnp.where(row == 0, prev_last, pltpu.roll(h, 1, 0)) - h

    def mix(i):
        return (h + xx * mu_ref[i:i + 1, :]).astype(BF16)

    xr, xw, xk, xv, xa, xg = [mix(i) for i in range(6)]
    r_out[0] = _dot(xr, wr_ref[...]).astype(BF16)
    k_out[0] = _dot(xk, wk_ref[...]).astype(BF16)
    v = _dot(xv, wv_ref[...])
    wl = w0_ref[...] + _dot(jnp.tanh(_dot(xw, w1_ref[...])).astype(BF16), w2_ref[...])
    w = -(jnp.maximum(-wl, 0.0) + jnp.log(1.0 + jnp.exp(-jnp.abs(wl)))) - 0.5
    lw_out[0] = -jnp.exp(w)
    a_out[0] = _sigmoid(a0_ref[...] + _dot(_dot(xa, a1_ref[...]).astype(BF16), a2_ref[...])).astype(BF16)
    g_out[0] = _dot(_sigmoid(_dot(xg, g1_ref[...])).astype(BF16), g2_ref[...]).astype(BF16)
    if has_vres:
        gate = _sigmoid(v0_ref[...] + _dot(_dot(xv, v1_ref[...]).astype(BF16), v2_ref[...]))
        v = v + (vf_ref[0].astype(F32) - v) * gate
    v_out[0] = v.astype(BF16)


def _rwkv_proj(x, gn, mu, wr, wk, wv, w0, w1, w2, a0, a1, a2, g1, g2, vres, v_first, tm=512):
    B, S, D = x.shape
    has_vres = vres is not None
    row = lambda a: a.reshape(1, -1)
    full = lambda a: pl.BlockSpec(a.shape, lambda b, s: (0,) * a.ndim, pipeline_mode=pl.Buffered(1))
    tok = pl.BlockSpec((1, tm, D), lambda b, s: (b, s, 0))
    ins = [x, row(gn), mu, wr.astype(BF16), wk.astype(BF16), wv.astype(BF16), row(w0),
           w1.astype(BF16), w2.astype(BF16), row(a0), a1.astype(BF16), a2.astype(BF16),
           g1.astype(BF16), g2.astype(BF16)]
    if has_vres:
        v0, v1, v2 = vres
        ins += [row(v0), v1.astype(BF16), v2.astype(BF16)]
    specs = [tok] + [full(a) for a in ins[1:]]
    if has_vres:
        ins.append(v_first)
        specs.append(tok)
    out_shape = [jax.ShapeDtypeStruct((B, S, D), dt) for dt in (BF16, F32, BF16, BF16, BF16, BF16)]
    return pl.pallas_call(
        functools.partial(_rwkv_proj_kernel, has_vres),
        grid=(B, S // tm),
        in_specs=specs,
        out_specs=[tok] * 6,
        out_shape=out_shape,
        scratch_shapes=[pltpu.VMEM((1, D), F32)],
        compiler_params=_params("arbitrary", "arbitrary"),
        name="rwkv_proj_vres" if has_vres else "rwkv_proj",
    )(*ins)


def _head_sum(x, ones_b):
    hi = x.astype(BF16)
    lo = (x - hi.astype(F32)).astype(BF16)
    return _dot(hi, ones_b) + _dot(lo, ones_b)


def _wkv_kernel(r_ref, lw_ref, k_ref, v_ref, a_ref, g_ref, kk_ref, ka_ref, rk_ref, lnw_ref, lnb_ref,
                z_ref, state):
    C = WKV_CHUNK
    P = 2 * C
    tt = r_ref.shape[1]
    nc = tt // C

    @pl.when(pl.program_id(2) == 0)
    def _():
        state[...] = jnp.zeros_like(state)

    lane = lax.broadcasted_iota(jnp.int32, (C, LANES), 1)
    m0 = (lane < HEAD_DIM).astype(F32)
    m1 = 1.0 - m0
    ri = lax.broadcasted_iota(jnp.int32, (P, P), 0)
    ci = lax.broadcasted_iota(jnp.int32, (P, P), 1)
    strict = ri > ci
    incl = ri >= ci
    eye = (ri == ci).astype(F32)
    tri = (lax.broadcasted_iota(jnp.int32, (C, C), 0) >= lax.broadcasted_iota(jnp.int32, (C, C), 1)).astype(F32)
    ones_b = _pair_ones().astype(BF16)

    def stack(x):
        return jnp.concatenate([x * m0, x * m1], axis=0).astype(BF16)

    nq = r_ref.shape[2] // LANES
    cs = range(nc)
    pre = []
    for q in range(nq):
        ls = slice(q * LANES, (q + 1) * LANES)
        r = r_ref[0, :, ls].astype(F32)
        k = k_ref[0, :, ls].astype(F32)
        v = v_ref[0, :, ls].astype(F32)
        a = a_ref[0, :, ls].astype(F32)
        lw = lw_ref[0, :, ls]
        kk = k * kk_ref[:, ls]
        kk = kk * lax.rsqrt(jnp.maximum(_head_sum(kk * kk, ones_b), 1e-24))
        k2 = k * (1.0 + (a - 1.0) * ka_ref[:, ls])
        bv = kk * a
        cum_c = _dot_hi(tri, jnp.concatenate([lw[c * C:(c + 1) * C] for c in cs], axis=1))
        cum = jnp.concatenate([cum_c[:, c * LANES:(c + 1) * LANES] for c in cs], axis=0)
        cum_end = [cum_c[C - 1:C, c * LANES:(c + 1) * LANES] for c in cs]
        to_end = jnp.concatenate([jnp.broadcast_to(ce, (C, LANES)) for ce in cum_end], axis=0) - cum
        g_inv = jnp.exp(-cum)
        g_end = jnp.exp(to_end)
        pre.append(dict(r=r, k2=k2, v=v, cum_end=cum_end, a_t=-kk * jnp.exp(cum - lw), r_t=r * jnp.exp(cum),
                        b_t=bv * g_inv, k_t=k2 * g_inv, b_h=bv * g_end, k_h=k2 * g_end))

    units = [(q, c) for c in cs for q in range(nq)]
    us = range(len(units))

    def stacks(name):
        return [stack(pre[q][name][c * C:(c + 1) * C]) for q, c in units]

    a_s, r_s, b_s, k_s, v_s, bh_s, kh_s = (stacks(x) for x in ("a_t", "r_t", "b_t", "k_t", "v", "b_h", "k_h"))
    cat0 = lambda x, y: jnp.concatenate([x, y], axis=0)
    cat1 = lambda x, y: jnp.concatenate([x, y], axis=1)
    gram = [_dot_nt(cat0(a_s[u], r_s[u]), cat0(b_s[u], k_s[u])) for u in us]
    n_ab = [jnp.where(strict, g[:P, :P], 0.0) for g in gram]
    a_ak = [jnp.where(strict, g[:P, P:], 0.0).astype(BF16) for g in gram]
    a_rb = [jnp.where(incl, g[P:, :P], 0.0).astype(BF16) for g in gram]
    a_rk = [jnp.where(incl, g[P:, P:], 0.0).astype(BF16) for g in gram]
    akv = [_dot(a_ak[u], v_s[u]).astype(BF16) for u in us]
    pw = [n.astype(BF16) for n in n_ab]
    inv = [eye + n for n in n_ab]
    pw = [_dot(p, p).astype(BF16) for p in pw]
    for _ in range(4):
        both = [_dot(cat0(pw[u], inv[u].astype(BF16)), pw[u]) for u in us]
        pw = [x[:P].astype(BF16) for x in both]
        inv = [inv[u] + both[u][P:] for u in us]
    inv = [(inv[u] + _dot(inv[u].astype(BF16), pw[u])).astype(BF16) for u in us]
    wu = [_dot(inv[u], cat1(a_s[u], akv[u])).astype(BF16) for u in us]
    rb = [_dot(a_rb[u], wu[u]) for u in us]
    q_h = [(r_s[u].astype(F32) + rb[u][:, :LANES]).astype(BF16) for u in us]
    y0 = [rb[u][:, LANES:] + _dot(a_rk[u], v_s[u]) for u in us]
    m_st = [_dot_tn(wu[u][:, :LANES], bh_s[u]).astype(BF16) for u in us]
    z_st = [_dot_tn(cat0(wu[u][:, LANES:], v_s[u]), cat0(bh_s[u], kh_s[u])) for u in us]

    s = [state[q] for q in range(nq)]
    ys = [[] for _ in range(nq)]
    for c in cs:
        for q in range(nq):
            u = c * nq + q
            sb = s[q].astype(BF16)
            y_s = _dot_nt(q_h[u], sb) + y0[u]
            ys[q].append(y_s[:C] + y_s[C:])
            s[q] = s[q] * jnp.exp(pre[q]["cum_end"][c]) + _dot(sb, m_st[u]) + z_st[u]
    for q in range(nq):
        state[q] = s[q]

    for q in range(nq):
        ls = slice(q * LANES, (q + 1) * LANES)
        y = jnp.concatenate(ys[q], axis=0)
        mean = _head_sum(y, ones_b) * (1.0 / HEAD_DIM)
        yc = y - mean
        var = _head_sum(yc * yc, ones_b) * (1.0 / HEAD_DIM)
        yn = yc * lax.rsqrt(var + GN_EPS) * lnw_ref[:, ls] + lnb_ref[:, ls]
        bonus = _head_sum(pre[q]["r"] * pre[q]["k2"] * rk_ref[:, ls], ones_b) * pre[q]["v"]
        z_ref[0, :, ls] = ((yn + bonus) * g_ref[0, :, ls].astype(F32)).astype(BF16)


def _wkv(r, lw, k, v, a, g, k_k, k_a, r_k, ln_w, ln_b):
    B, S, D = r.shape
    tt = min(WKV_TILE, S)
    width = WKV_PAIRS * LANES
    tok = pl.BlockSpec((1, tt, width), lambda b, p, t: (b, t, p))
    par = pl.BlockSpec((1, width), lambda b, p, t: (0, p))
    row = lambda x: x.reshape(1, D)
    return pl.pallas_call(
        _wkv_kernel,
        grid=(B, D // width, S // tt),
        in_specs=[tok] * 6 + [par] * 5,
        out_specs=tok,
        out_shape=jax.ShapeDtypeStruct((B, S, D), BF16),
        scratch_shapes=[pltpu.VMEM((WKV_PAIRS, LANES, LANES), F32)],
        compiler_params=_params("arbitrary", "arbitrary", "arbitrary"),
        name="wkv7",
    )(r, lw, k, v, a, g, row(k_k), row(k_a), row(r_k), row(ln_w), row(ln_b))


def _proj_res_kernel(a_ref, w_ref, res_ref, o_ref):
    o_ref[...] = res_ref[...] + _dot(a_ref[...], w_ref[...])


def _proj_res(a, w, res, tm=512):
    n, kdim = a.shape
    d = w.shape[1]
    return pl.pallas_call(
        _proj_res_kernel,
        grid=(n // tm,),
        in_specs=[pl.BlockSpec((tm, kdim), lambda i: (i, 0)), pl.BlockSpec((kdim, d), lambda i: (0, 0)),
                  pl.BlockSpec((tm, d), lambda i: (i, 0))],
        out_specs=pl.BlockSpec((tm, d), lambda i: (i, 0)),
        out_shape=jax.ShapeDtypeStruct((n, d), F32),
        compiler_params=_params("arbitrary"),
        name="proj_res",
    )(a, w.astype(BF16), res)


def _ffn_kernel(x_ref, gn_ref, w1_ref, w3_ref, w2_ref, o_ref, h_ref, acc_ref):
    f = pl.program_id(1)

    @pl.when(f == 0)
    def _():
        h_ref[...] = _rms(x_ref[...], gn_ref[...]).astype(BF16)

    h = h_ref[...]
    h1 = _dot(h, w1_ref[...])
    act = (h1 * _sigmoid(h1) * _dot(h, w3_ref[...])).astype(BF16)
    part = _dot(act, w2_ref[...])

    @pl.when(f == 0)
    def _():
        acc_ref[...] = part

    @pl.when(f > 0)
    def _():
        acc_ref[...] += part

    @pl.when(f == pl.num_programs(1) - 1)
    def _():
        o_ref[...] = x_ref[...] + acc_ref[...]


def _ffn(x, gn, w1, w3, w2, tm=512, tf=1408):
    n, d = x.shape
    fdim = w1.shape[1]
    return pl.pallas_call(
        _ffn_kernel,
        grid=(n // tm, fdim // tf),
        in_specs=[pl.BlockSpec((tm, d), lambda i, f: (i, 0)),
                  pl.BlockSpec((1, d), lambda i, f: (0, 0)),
                  pl.BlockSpec((d, tf), lambda i, f: (0, f)),
                  pl.BlockSpec((d, tf), lambda i, f: (0, f)),
                  pl.BlockSpec((tf, d), lambda i, f: (f, 0))],
        out_specs=pl.BlockSpec((tm, d), lambda i, f: (i, 0)),
        out_shape=jax.ShapeDtypeStruct((n, d), F32),
        scratch_shapes=[pltpu.VMEM((tm, d), BF16), pltpu.VMEM((tm, d), F32)],
        compiler_params=_params("arbitrary", "arbitrary"),
        name="ffn_dense",
    )(x, gn.reshape(1, d), w1.astype(BF16), w3.astype(BF16), w2.astype(BF16))


def _router_kernel(x_ref, gn_ref, wr_ref, h_ref, route_ref, cnt_ref, carry):
    tm = x_ref.shape[0]

    @pl.when(pl.program_id(0) == 0)
    def _():
        carry[...] = jnp.zeros_like(carry)

    h = _rms(x_ref[...], gn_ref[...])
    h_ref[...] = h
    logits = _dot_hi(h, wr_ref[...])
    lane = lax.broadcasted_iota(jnp.int32, logits.shape, 1)
    neg = jnp.float32(-jnp.inf)
    logits = jnp.where(lane < N_EXPERTS, logits, neg)
    m1 = jnp.max(logits, axis=-1, keepdims=True)
    i1 = jnp.min(jnp.where(logits == m1, lane, LANES), axis=-1, keepdims=True)
    rest = jnp.where(lane == i1, neg, logits)
    m2 = jnp.max(rest, axis=-1, keepdims=True)
    i2 = jnp.min(jnp.where(rest == m2, lane, LANES), axis=-1, keepdims=True)
    e2 = jnp.exp(m2 - m1)
    p1 = 1.0 / (1.0 + e2)
    p2 = e2 * p1
    oh1 = (lane == i1).astype(F32)
    oh2 = (lane == i2).astype(F32)
    cnt = oh1 + oh2
    earlier = (lax.broadcasted_iota(jnp.int32, (tm, tm), 0) > lax.broadcasted_iota(jnp.int32, (tm, tm), 1))
    before = carry[...] + _dot(earlier.astype(BF16), cnt.astype(BF16))
    r1 = jnp.sum(oh1 * before, axis=-1, keepdims=True)
    r2 = jnp.sum(oh2 * before, axis=-1, keepdims=True)
    total = carry[...] + jnp.sum(cnt, axis=0, keepdims=True)
    carry[...] = total
    cnt_ref[...] = total
    cols = (i1.astype(F32), i2.astype(F32), p1, p2, r1, r2)
    route = jnp.zeros(logits.shape, F32)
    for c, val in enumerate(cols):
        route = jnp.where(lane == c, val, route)
    route_ref[...] = route


def _router(x, gn, w_router, tm=512):
    n, d = x.shape
    wr = jnp.zeros((d, LANES), F32).at[:, :N_EXPERTS].set(w_router)
    return pl.pallas_call(
        _router_kernel,
        grid=(n // tm,),
        in_specs=[pl.BlockSpec((tm, d), lambda i: (i, 0)),
                  pl.BlockSpec((1, d), lambda i: (0, 0)),
                  pl.BlockSpec((d, LANES), lambda i: (0, 0))],
        out_specs=[pl.BlockSpec((tm, d), lambda i: (i, 0)), pl.BlockSpec((tm, LANES), lambda i: (i, 0)),
                   pl.BlockSpec((1, LANES), lambda i: (0, 0))],
        out_shape=[jax.ShapeDtypeStruct((n, d), F32), jax.ShapeDtypeStruct((n, LANES), F32),
                   jax.ShapeDtypeStruct((1, LANES), F32)],
        scratch_shapes=[pltpu.VMEM((1, LANES), F32)],
        compiler_params=_params("arbitrary"),
        name="moe_router",
    )(x, gn.reshape(1, d), wr)


def _dispatch_kernel(dest_ref, pend_ref, h_ref, xs_ref, zeros, sem, zsem):
    i = pl.program_id(0)
    tm = h_ref.shape[0]
    n = pl.num_programs(0) * tm

    @pl.when(i == 0)
    def _():
        zeros[...] = jnp.zeros_like(zeros)

        def tail(e):
            start = pl.multiple_of(pend_ref[e] - MOE_BM, MOE_BM)
            return pltpu.make_async_copy(zeros, xs_ref.at[pl.ds(start, MOE_BM)], zsem)

        def nonempty(e):
            return pend_ref[e] > (pend_ref[e - 1] if e else 0)

        for e in range(N_EXPERTS):
            pl.when(nonempty(e))(lambda e=e: tail(e).start())
        for e in range(N_EXPERTS):
            pl.when(nonempty(e))(lambda e=e: tail(e).wait())

        def spare(b):
            return pltpu.make_async_copy(zeros, xs_ref.at[pl.ds(pl.multiple_of(b * MOE_BM, MOE_BM), MOE_BM)], zsem)

        first_spare = pend_ref[N_EXPERTS - 1] // MOE_BM
        n_blocks = xs_ref.shape[0] // MOE_BM
        lax.fori_loop(first_spare, n_blocks, lambda b, c: (spare(b).start(), c)[1], 0)
        lax.fori_loop(first_spare, n_blocks, lambda b, c: (spare(b).wait(), c)[1], 0)

    def row_copy(j, k):
        return pltpu.make_async_copy(h_ref.at[pl.ds(j, 1)],
                                     xs_ref.at[pl.ds(dest_ref[k * n + i * tm + j], 1)], sem)

    def start(j, c):
        row_copy(j, 0).start()
        row_copy(j, 1).start()
        return c

    def wait(j, c):
        row_copy(j, 0).wait()
        row_copy(j, 1).wait()
        return c

    lax.fori_loop(0, tm, start, 0, unroll=8)
    lax.fori_loop(0, tm, wait, 0, unroll=8)


def _dispatch(h, dest, pend, cap, tm=512):
    n, d = h.shape
    return pl.pallas_call(
        _dispatch_kernel,
        grid_spec=pltpu.PrefetchScalarGridSpec(
            num_scalar_prefetch=2,
            grid=(n // tm,),
            in_specs=[pl.BlockSpec((tm, d), lambda i, dest, pend: (i, 0))],
            out_specs=pl.BlockSpec(memory_space=pl.ANY),
            scratch_shapes=[pltpu.VMEM((MOE_BM, d), F32), pltpu.SemaphoreType.DMA(()),
                            pltpu.SemaphoreType.DMA(())],
        ),
        out_shape=jax.ShapeDtypeStruct((cap, d), F32),
        compiler_params=_params("arbitrary"),
        name="moe_dispatch",
    )(dest, pend, h)


def _experts_kernel(be_ref, nu_ref, x_ref, w1_ref, w3_ref, w2_ref, o_ref, xb_ref, acc_ref):
    i, f = pl.program_id(0), pl.program_id(1)
    used = i < nu_ref[0]

    @pl.when(jnp.logical_and(used, f == 0))
    def _():
        xb_ref[...] = x_ref[...].astype(BF16)

    @pl.when(used)
    def _():
        x = xb_ref[...]
        tf = w1_ref.shape[2]
        part = None
        for c in range(0, tf, MOE_SLAB):
            h1 = _dot(x, w1_ref[0, :, c:c + MOE_SLAB])
            act = (h1 * _sigmoid(h1) * _dot(x, w3_ref[0, :, c:c + MOE_SLAB])).astype(BF16)
            down = _dot(act, w2_ref[0, c:c + MOE_SLAB, :])
            part = down if part is None else part + down

        @pl.when(f == 0)
        def _():
            acc_ref[...] = part

        @pl.when(f > 0)
        def _():
            acc_ref[...] += part

    @pl.when(f == pl.num_programs(1) - 1)
    def _():
        o_ref[...] = jnp.where(used, acc_ref[...], 0.0)


def _experts(xs, blk_e, n_used, w1, w3, w2, tf=MOE_TF):
    cap, d = xs.shape
    edim = w1.shape[2]
    nb = cap // MOE_BM
    return pl.pallas_call(
        _experts_kernel,
        grid_spec=pltpu.PrefetchScalarGridSpec(
            num_scalar_prefetch=2,
            grid=(nb, edim // tf),
            in_specs=[pl.BlockSpec((MOE_BM, d), lambda i, f, be, nu: (jnp.minimum(i, nu[0] - 1), 0)),
                      pl.BlockSpec((1, d, tf), lambda i, f, be, nu: (be[i], 0, f)),
                      pl.BlockSpec((1, d, tf), lambda i, f, be, nu: (be[i], 0, f)),
                      pl.BlockSpec((1, tf, d), lambda i, f, be, nu: (be[i], f, 0))],
            out_specs=pl.BlockSpec((MOE_BM, d), lambda i, f, be, nu: (i, 0)),
            scratch_shapes=[pltpu.VMEM((MOE_BM, d), BF16), pltpu.VMEM((MOE_BM, d), F32)],
        ),
        out_shape=jax.ShapeDtypeStruct((cap, d), F32),
        compiler_params=_params("arbitrary", "arbitrary"),
        name="moe_experts",
    )(blk_e, n_used, xs, w1.astype(BF16), w3.astype(BF16), w2.astype(BF16))


def _combine_kernel(dest_ref, x_ref, route_ref, yb_ref, o_ref, buf, sem):
    i = pl.program_id(0)
    nt = pl.num_programs(0)
    tm = x_ref.shape[0]
    n = nt * tm

    def row_copy(tile, slot, j, k):
        return pltpu.make_async_copy(yb_ref.at[pl.ds(dest_ref[k * n + tile * tm + j], 1)],
                                     buf.at[slot, k, pl.ds(j, 1)], sem.at[slot])

    def fetch(tile, slot):
        def body(j, c):
            row_copy(tile, slot, j, 0).start()
            row_copy(tile, slot, j, 1).start()
            return c
        lax.fori_loop(0, tm, body, 0, unroll=8)

    def drain(tile, slot):
        def body(j, c):
            row_copy(tile, slot, j, 0).wait()
            row_copy(tile, slot, j, 1).wait()
            return c
        lax.fori_loop(0, tm, body, 0, unroll=8)

    slot = i % 2

    @pl.when(i == 0)
    def _():
        fetch(0, 0)

    @pl.when(i + 1 < nt)
    def _():
        fetch(i + 1, 1 - slot)

    drain(i, slot)
    p1 = route_ref[:, 2:3]
    p2 = route_ref[:, 3:4]
    o_ref[...] = x_ref[...] + (p1 * buf[slot, 0] + p2 * buf[slot, 1])


def _combine(x, route, yb, dest, tm=512):
    n, d = x.shape
    return pl.pallas_call(
        _combine_kernel,
        grid_spec=pltpu.PrefetchScalarGridSpec(
            num_scalar_prefetch=1,
            grid=(n // tm,),
            in_specs=[pl.BlockSpec((tm, d), lambda i, dest: (i, 0)),
                      pl.BlockSpec((tm, LANES), lambda i, dest: (i, 0)),
                      pl.BlockSpec(memory_space=pl.ANY)],
            out_specs=pl.BlockSpec((tm, d), lambda i, dest: (i, 0)),
            scratch_shapes=[pltpu.VMEM((2, 2, tm, d), F32), pltpu.SemaphoreType.DMA((2,))],
        ),
        out_shape=jax.ShapeDtypeStruct((n, d), F32),
        compiler_params=_params("arbitrary"),
        name="moe_combine",
    )(dest, x, route, yb)


def _moe(x, gn, w_router, w1, w3, w2):
    n, d = x.shape
    h, route, cnt = _router(x, gn, w_router)
    counts = cnt[0, :N_EXPERTS].astype(jnp.int32)
    padded = ((counts + MOE_BM - 1) // MOE_BM) * MOE_BM
    pend = jnp.cumsum(padded).astype(jnp.int32)
    pstart = pend - padded
    experts = jnp.arange(N_EXPERTS, dtype=jnp.int32)

    def dest_of(e_col, r_col):
        e = route[:, e_col].astype(jnp.int32)
        start = jnp.sum(jnp.where(e[:, None] == experts[None, :], pstart[None, :], 0), axis=1)
        return start + route[:, r_col].astype(jnp.int32)

    dest = jnp.concatenate([dest_of(0, 4), dest_of(1, 5)])
    nb = -(-(2 * n) // MOE_BM) + N_EXPERTS
    blk_e = jnp.minimum(jnp.searchsorted(pend, jnp.arange(nb, dtype=jnp.int32) * MOE_BM, side="right"),
                        N_EXPERTS - 1).astype(jnp.int32)
    n_used = (pend[-1:] // MOE_BM).astype(jnp.int32)
    xs = _dispatch(h, dest, pend, nb * MOE_BM)
    yb = _experts(xs, blk_e, n_used, w1, w3, w2)
    return _combine(x, route, yb, dest)


def _kv_kernel(x_ref, gn_ref, wk_ref, wvt_ref, wf_ref, bf_ref, kn_ref, sel_ref, k_out, v_out, carry):
    s = pl.program_id(1)
    tm = x_ref.shape[1]

    @pl.when(s == 0)
    def _():
        carry[...] = jnp.zeros_like(carry)

    h = _rms(x_ref[0], gn_ref[...])
    hb = h.astype(BF16)
    k = _head_rms(_dot(hb, wk_ref[...]), kn_ref[...])
    v_out[0, :, :HEAD_DIM, :] = _dot_nt(wvt_ref[...], hb).reshape(N_HEADS, HEAD_DIM, tm).astype(BF16)
    row = lax.broadcasted_iota(jnp.int32, (N_HEADS, V_ROWS - HEAD_DIM, tm), 1)
    v_out[0, :, HEAD_DIM:, :] = (row == 0).astype(BF16)
    fl = _dot(hb, wf_ref[...]) + bf_ref[...]
    log_f = jnp.minimum(fl, 0.0) - jnp.log(1.0 + jnp.exp(-jnp.abs(fl)))
    tri = (lax.broadcasted_iota(jnp.int32, (tm, tm), 0) >= lax.broadcasted_iota(jnp.int32, (tm, tm), 1)).astype(F32)
    dcum = carry[...] + _dot_hi(tri, log_f)
    carry[...] = dcum[tm - 1:tm, :]
    rest = -dcum * LOG2_E
    pieces = []
    for _ in range(N_BIAS):
        piece = rest.astype(BF16)
        pieces.append(piece)
        rest = rest - piece.astype(F32)
    pieces = jnp.concatenate(pieces, axis=1)
    lane = lax.broadcasted_iota(jnp.int32, (tm, LANES), 1)
    for hd in range(N_HEADS):
        tile = k[:, (hd // 2) * LANES:(hd // 2 + 1) * LANES]
        if hd % 2:
            tile = pltpu.roll(tile, HEAD_DIM, 1)
        bias = _dot(pieces, sel_ref[hd])
        k_out[0, hd] = jnp.where(lane < HEAD_DIM, tile, bias).astype(BF16)


def _shared_kv(x, gn, w_kvf, b_f, k_norm, tm=256):
    B, S, D = x.shape
    wk = w_kvf[:, :D].astype(BF16)
    wvt = w_kvf[:, D:2 * D].T.astype(BF16)
    wf = jnp.zeros((D, LANES), F32).at[:, :N_HEADS].set(w_kvf[:, 2 * D:]).astype(BF16)
    bf = jnp.zeros((1, LANES), F32).at[0, :N_HEADS].set(b_f)
    kn = jnp.tile(k_norm, N_HEADS).reshape(1, D)
    hh = jnp.arange(N_HEADS)[:, None, None]
    rr = jnp.arange(N_BIAS * LANES)[None, :, None]
    cc = jnp.arange(LANES)[None, None, :]
    sel = ((rr % LANES == hh) & (cc == HEAD_DIM + rr // LANES)).astype(BF16)
    tok = pl.BlockSpec((1, tm, D), lambda b, s: (b, s, 0))
    full = lambda a: pl.BlockSpec(a.shape, lambda b, s: (0,) * a.ndim)
    ins = [x, gn.reshape(1, D), wk, wvt, wf, bf, kn, sel]
    return pl.pallas_call(
        _kv_kernel,
        grid=(B, S // tm),
        in_specs=[tok] + [full(a) for a in ins[1:]],
        out_specs=[pl.BlockSpec((1, N_HEADS, tm, LANES), lambda b, s: (b, 0, s, 0)),
                   pl.BlockSpec((1, N_HEADS, V_ROWS, tm), lambda b, s: (b, 0, 0, s))],
        out_shape=[jax.ShapeDtypeStruct((B, N_HEADS, S, LANES), BF16),
                   jax.ShapeDtypeStruct((B, N_HEADS, V_ROWS, S), BF16)],
        scratch_shapes=[pltpu.VMEM((1, LANES), F32)],
        compiler_params=_params("arbitrary", "arbitrary"),
        name="shared_kv",
    )(*ins)


def _qg_kernel(x_ref, gn_ref, wqt_ref, wgt_ref, qn_ref, q_out, og_out):
    tm = x_ref.shape[1]
    hb = _rms(x_ref[0], gn_ref[...]).astype(BF16)
    q = _dot_nt(wqt_ref[...], hb).reshape(N_HEADS, HEAD_DIM, tm)
    q = q * lax.rsqrt(jnp.mean(q * q, axis=1, keepdims=True) + RMS_EPS) * qn_ref[...]
    q_out[0, :, :HEAD_DIM, :] = (q * (HEAD_DIM ** -0.5 * LOG2_E)).astype(BF16)
    row = lax.broadcasted_iota(jnp.int32, (N_HEADS, LANES - HEAD_DIM, tm), 1)
    q_out[0, :, HEAD_DIM:, :] = (row < N_BIAS).astype(BF16)
    og_out[0] = _dot_nt(wgt_ref[...], hb).reshape(N_HEADS, HEAD_DIM, tm).astype(BF16)


def _qg_proj(x, gn, w_qg, q_norm, tm=512):
    B, S, D = x.shape
    ins = [x, gn.reshape(1, D), w_qg[:, :D].T.astype(BF16), w_qg[:, D:].T.astype(BF16),
           q_norm.reshape(1, HEAD_DIM, 1)]
    const = lambda a: pl.BlockSpec(a.shape, lambda b, s: (0,) * a.ndim)
    return pl.pallas_call(
        _qg_kernel,
        grid=(B, S // tm),
        in_specs=[pl.BlockSpec((1, tm, D), lambda b, s: (b, s, 0))] + [const(a) for a in ins[1:]],
        out_specs=[pl.BlockSpec((1, N_HEADS, LANES, tm), lambda b, s: (b, 0, 0, s)),
                   pl.BlockSpec((1, N_HEADS, HEAD_DIM, tm), lambda b, s: (b, 0, 0, s))],
        out_shape=[jax.ShapeDtypeStruct((B, N_HEADS, LANES, S), BF16),
                   jax.ShapeDtypeStruct((B, N_HEADS, HEAD_DIM, S), BF16)],
        compiler_params=_params("arbitrary", "arbitrary"),
        name="fox_qg",
    )(*ins)


def _proj_res_t_kernel(a_ref, gate_ref, w_ref, res_ref, o_ref):
    tm = res_ref.shape[1]
    a = (a_ref[0].astype(F32) * _sigmoid(gate_ref[0].astype(F32))).astype(BF16)
    o_ref[0] = res_ref[0] + _dot_tn(a.reshape(D_MODEL, tm), w_ref[...])


def _proj_res_t(a_t, gate_t, w, res, tm=512):
    B, S, D = res.shape
    head_t = pl.BlockSpec((1, N_HEADS, HEAD_DIM, tm), lambda b, s: (b, 0, 0, s))
    tok = pl.BlockSpec((1, tm, D), lambda b, s: (b, s, 0))
    return pl.pallas_call(
        _proj_res_t_kernel,
        grid=(B, S // tm),
        in_specs=[head_t, head_t, pl.BlockSpec((D, D), lambda b, s: (0, 0)), tok],
        out_specs=tok,
        out_shape=jax.ShapeDtypeStruct((B, S, D), F32),
        compiler_params=_params("arbitrary", "arbitrary"),
        name="proj_res_gated",
    )(a_t, gate_t, w.astype(BF16), res)


def _attn_kernel(qt_ref, k_ref, vt_ref, o_ref):
    tq, tk = ATT_TQ, ATT_TK
    i = pl.program_id(2)
    hs = range(ATT_HEADS)
    qt = [qt_ref[0, h] for h in hs]

    def block(carry, off, nk, q_lo, masked):
        qs = slice(q_lo, tq)
        s = [_dot(k_ref[0, h, pl.ds(off, nk), :], qt[h][:, qs]) for h in hs]
        if masked:
            shape = (nk, tq - q_lo)
            key = off + lax.broadcasted_iota(jnp.int32, shape, 0)
            qry = i * tq + q_lo + lax.broadcasted_iota(jnp.int32, shape, 1)
            s = [jnp.where(key <= qry, x, -jnp.inf) for x in s]
        m_old = [carry[h][0][:, qs] for h in hs]
        m_new = [jnp.maximum(m_old[h], jnp.max(s[h], axis=0, keepdims=True)) for h in hs]
        alpha = [jnp.exp2(m_old[h] - m_new[h]) for h in hs]
        p = [jnp.exp2(s[h] - m_new[h]).astype(BF16) for h in hs]
        pv = [_dot(vt_ref[0, h, :, pl.ds(off, nk)], p[h]) for h in hs]
        acc = [alpha[h] * carry[h][1][:, qs] + pv[h] for h in hs]
        if q_lo:
            m_new = [jnp.concatenate([carry[h][0][:, :q_lo], m_new[h]], axis=1) for h in hs]
            acc = [jnp.concatenate([carry[h][1][:, :q_lo], acc[h]], axis=1) for h in hs]
        return tuple((m_new[h], acc[h]) for h in hs)

    init = tuple((jnp.full((1, tq), -jnp.inf, F32), jnp.zeros((V_ROWS, tq), F32)) for _ in hs)
    carry = lax.fori_loop(0, i, lambda j, c: block(c, pl.multiple_of(j * tk, tk), tk, 0, False), init)
    diag = pl.multiple_of(i * tk, tk)
    carry = block(carry, diag, tk // 2, 0, True)
    carry = block(carry, diag + tk // 2, tk // 2, tq // 2, True)
    for h in hs:
        acc = carry[h][1]
        o_ref[0, h] = (acc[:HEAD_DIM] / acc[HEAD_DIM:HEAD_DIM + 1]).astype(BF16)


def _fox_attn(qt, k_aug, vt):
    B, H, _, S = qt.shape
    g = ATT_HEADS
    return pl.pallas_call(
        _attn_kernel,
        grid=(B, H // g, S // ATT_TQ),
        in_specs=[pl.BlockSpec((1, g, LANES, ATT_TQ), lambda b, h, i: (b, h, 0, i)),
                  pl.BlockSpec((1, g, S, LANES), lambda b, h, i: (b, h, 0, 0)),
                  pl.BlockSpec((1, g, V_ROWS, S), lambda b, h, i: (b, h, 0, 0))],
        out_specs=pl.BlockSpec((1, g, HEAD_DIM, ATT_TQ), lambda b, h, i: (b, h, 0, i)),
        out_shape=jax.ShapeDtypeStruct((B, H, HEAD_DIM, S), BF16),
        compiler_params=_params("arbitrary", "arbitrary", "arbitrary"),
        name="fox_attn",
    )(qt, k_aug, vt)


def kernel(x, norm_mix, norm_ffn, rk_mu, rk_w_r, rk_w_k, rk_w_v, rk_w_o, rk_w0, rk_w1, rk_w2, rk_a0, rk_a1, rk_a2, rk_v0, rk_v1, rk_v2, rk_g1, rk_g2, rk_k_k, rk_k_a, rk_r_k, rk_lnx_w, rk_lnx_b, kv_norm, w_kvf, b_f, k_norm, fx_w_qg, fx_q_norm, fx_w_o, ffn_w1, ffn_w3, ffn_w2, moe_router, moe_w1, moe_w3, moe_w2):
    B, S, D = x.shape
    n = B * S
    depth = norm_mix.shape[0]
    n_rwkv = rk_mu.shape[0]
    v_first = None
    kv = None
    for l in range(depth):
        if l < n_rwkv:
            vres = None if l == 0 else (rk_v0[l - 1], rk_v1[l - 1], rk_v2[l - 1])
            r, lw, k, v, a, g = _rwkv_proj(
                x, norm_mix[l], rk_mu[l], rk_w_r[l], rk_w_k[l], rk_w_v[l], rk_w0[l], rk_w1[l], rk_w2[l],
                rk_a0[l], rk_a1[l], rk_a2[l], rk_g1[l], rk_g2[l], vres, v_first)
            if l == 0:
                v_first = v
            z = _wkv(r, lw, k, v, a, g, rk_k_k[l], rk_k_a[l], rk_r_k[l], rk_lnx_w[l], rk_lnx_b[l])
            x2 = _proj_res(z.reshape(n, D), rk_w_o[l], x.reshape(n, D))
        else:
            if kv is None:
                kv = _shared_kv(x, kv_norm, w_kvf, b_f, k_norm)
            j = l - n_rwkv
            q_t, og_t = _qg_proj(x, norm_mix[l], fx_w_qg[j], fx_q_norm[j])
            o_t = _fox_attn(q_t, *kv)
            x2 = _proj_res_t(o_t, og_t, fx_w_o[j], x).reshape(n, D)
        i = l // 2
        if l % 2 == 0:
            x2 = _ffn(x2, norm_ffn[l], ffn_w1[i], ffn_w3[i], ffn_w2[i])
        else:
            x2 = _moe(x2, norm_ffn[l], moe_router[i], moe_w1[i], moe_w3[i], moe_w2[i])
        x = x2.reshape(B, S, D)
    return x
```
